```python
import math
import jax
import jax.numpy as jnp
from jax import lax
import numpy as np

D_MODEL = 4096
BATCH = 32
SEQ = 256
DEPTH = 4
DEC_BATCH = 2
DEC_SEQ = 1024
PAST_LEN = 512

GRID_W = 64
N_MIXERS = 2
N_MLSTM = (DEPTH + 1) // 2
N_S5 = DEPTH // 2
D_FF = 11008
NORM_EPS = 1e-6
N_SUB = 3
N_MOD = 3
M_HEADS = 8
M_QK = D_MODEL // 2
M_V = D_MODEL
M_DK = M_QK // M_HEADS
M_DV = M_V // M_HEADS
M_CHUNK = 64
M_CONV = 3
M_GATES = 4
M_PROJ = 2 * M_QK + 2 * M_V + M_GATES * M_HEADS
S5_WIDTH = D_MODEL
S5_GC = 16
S5_GROUPS = S5_WIDTH // S5_GC
S5_P = 64
DT_MIN = 1e-3
DT_MAX = 1e-1

kernel_name = 'hybrid_mlstm_s5_prefix_diffusion_step'


def rms_norm(x, g):
    x32 = x.astype(jnp.float32)
    y = x32 * lax.rsqrt(jnp.mean(x32 * x32, axis=-1, keepdims=True) + NORM_EPS)
    return (y * g.astype(jnp.float32)).astype(x.dtype)


def modulate(h, m):
    return h * (1 + m[:, 1]) + m[:, 0]


def swiglu(h, w_in, w_out):
    g, u = jnp.split(h @ w_in, 2, axis=-1)
    return (jax.nn.silu(g) * u) @ w_out


def short_conv(x, w, b, rows):
    bsz, t_, ch = x.shape
    if rows is None:
        xi = x.reshape(bsz, 1, t_, ch)
        k = w[1:2]
    else:
        xi = x.reshape(bsz, rows, GRID_W, ch)
        k = w
    y = lax.conv_general_dilated(xi, k[:, :, None, :].astype(x.dtype), (1, 1), 'SAME',
                                 dimension_numbers=('NHWC', 'HWIO', 'NHWC'),
                                 feature_group_count=ch)
    return y.reshape(bsz, t_, ch) + b.astype(x.dtype)


def mlstm_scan(q, k, v, log_i, log_f, c0, n0, m0):
    bsz, t_, nh, _ = q.shape
    nc = t_ // M_CHUNK

    def to_chunks(a):
        a = a.reshape((bsz, nc, M_CHUNK) + a.shape[2:])
        return jnp.moveaxis(jnp.moveaxis(a, 3, 2), 1, 0)

    lower = jnp.tril(jnp.ones((M_CHUNK, M_CHUNK), dtype=bool))

    def step(carry, xs):
        c, n, m = carry
        qc, kc, vc, li, lf = xs
        b = jnp.cumsum(lf, axis=-1)
        log_d = jnp.where(lower, b[..., :, None] - b[..., None, :] + li[..., None, :], -jnp.inf)
        log_inter = b + m[..., None]
        m_out = jnp.maximum(log_inter, jnp.max(log_d, axis=-1))
        dmat = jnp.exp(log_d - m_out[..., None])
        s_inter = jnp.exp(log_inter - m_out)
        s = jnp.einsum('bhtd,bhsd->bhts', qc, kc) * dmat
        num = (jnp.einsum('bhts,bhsv->bhtv', s, vc)
               + s_inter[..., None] * jnp.einsum('bhtd,bhdv->bhtv', qc, c))
        den = jnp.sum(s, axis=-1) + s_inter * jnp.einsum('bhtd,bhd->bht', qc, n)
        h = num / jnp.maximum(jnp.abs(den), jnp.exp(-m_out))[..., None]
        b_last = b[..., -1]
        log_w = b_last[..., None] - b + li
        m_new = jnp.maximum(b_last + m, jnp.max(log_w, axis=-1))
        w = jnp.exp(log_w - m_new[..., None])
        decay = jnp.exp(b_last + m - m_new)
        c_new = decay[..., None, None] * c + jnp.einsum('bhs,bhsd,bhsv->bhdv', w, kc, vc)
        n_new = decay[..., None] * n + jnp.einsum('bhs,bhsd->bhd', w, kc)
        return (c_new, n_new, m_new), h

    f32 = jnp.float32
    carry0 = (c0.astype(f32), n0.astype(f32), m0.astype(f32))
    xs = (to_chunks(q), to_chunks(k), to_chunks(v), to_chunks(log_i), to_chunks(log_f))
    fin, hs = lax.scan(step, carry0, xs)
    hs = jnp.moveaxis(jnp.moveaxis(hs, 0, 1), 2, 3).reshape(bsz, t_, nh, -1)
    return hs, fin


def mlstm_mixer(h, w_in, conv_w, conv_b, gate_b, head_g, w_out, state, rows):
    bsz, t_, _ = h.shape
    f32 = jnp.float32
    proj = h @ w_in
    qk = jax.nn.silu(short_conv(proj[..., :2 * M_QK], conv_w, conv_b, rows))
    v = proj[..., 2 * M_QK:2 * M_QK + M_V]
    o = proj[..., 2 * M_QK + M_V:2 * M_QK + 2 * M_V]
    g = (proj[..., 2 * M_QK + 2 * M_V:].reshape(bsz, t_, M_GATES, M_HEADS) + gate_b).astype(f32)
    q = qk[..., :M_QK].reshape(bsz, t_, M_HEADS, M_DK).astype(f32) * (M_DK ** -0.5)
    k = qk[..., M_QK:].reshape(bsz, t_, M_HEADS, M_DK).astype(f32)
    v = v.reshape(bsz, t_, M_HEADS, M_DV).astype(f32)
    c0, n0, m0 = state
    h_f, fin_f = mlstm_scan(q, k, v, g[:, :, 0], jax.nn.log_sigmoid(g[:, :, 1]),
                            c0[:, 0], n0[:, 0], m0[:, 0])
    flip = lambda a: jnp.flip(a, axis=1)
    h_b, fin_b = mlstm_scan(flip(q), flip(k), flip(v), flip(g[:, :, 2]),
                            flip(jax.nn.log_sigmoid(g[:, :, 3])), c0[:, 1], n0[:, 1], m0[:, 1])
    hh = rms_norm(h_f + flip(h_b), head_g.reshape(M_HEADS, M_DV)).reshape(bsz, t_, M_V)
    y = (hh.astype(h.dtype) * jax.nn.sigmoid(o)) @ w_out
    new_state = tuple(jnp.stack([a, b], axis=1).astype(h.dtype) for a, b in zip(fin_f, fin_b))
    return y, new_state


def s5_discretize(lam_re, lam_im, log_dt):
    f32 = jnp.float32
    lr = jnp.minimum(lam_re.astype(f32), -1e-4)
    li = lam_im.astype(f32)
    dt = jnp.exp(log_dt.astype(f32))[:, None]
    mag = jnp.exp(lr * dt)
    ar = mag * jnp.cos(li * dt)
    ai = mag * jnp.sin(li * dt)
    den = lr * lr + li * li
    xr = ar - 1.0
    cr = (xr * lr + ai * li) / den
    ci = (ai * lr - xr * li) / den
    return ar, ai, cr, ci


def cplx_combine(e1, e2):
    a1r, a1i, b1r, b1i = e1
    a2r, a2i, b2r, b2i = e2
    return (a1r * a2r - a1i * a2i, a1r * a2i + a1i * a2r,
            a2r * b1r - a2i * b1i + b2r, a2r * b1i + a2i * b1r + b2i)


def s5_direction(u, lam_re, lam_im, log_dt, b_re, b_im, c_re, c_im, x0r, x0i):
    f32 = jnp.float32
    ar, ai, cr, ci = s5_discretize(lam_re, lam_im, log_dt)
    bur = jnp.einsum('btgc,gpc->btgp', u, b_re.astype(f32))
    bui = jnp.einsum('btgc,gpc->btgp', u, b_im.astype(f32))
    er = cr * bur - ci * bui
    ei = cr * bui + ci * bur
    x0r = x0r.astype(f32)
    x0i = x0i.astype(f32)
    er = er.at[:, 0].add(ar * x0r - ai * x0i)
    ei = ei.at[:, 0].add(ar * x0i + ai * x0r)
    t_ = u.shape[1]
    a_r = jnp.broadcast_to(ar, (t_,) + ar.shape)
    a_i = jnp.broadcast_to(ai, (t_,) + ai.shape)

    def scan_one(er_, ei_):
        return lax.associative_scan(cplx_combine, (a_r, a_i, er_, ei_), axis=0)[2:]

    xr, xi = jax.vmap(scan_one)(er, ei)
    y = (jnp.einsum('btgp,gcp->btgc', xr, c_re.astype(f32))
         - jnp.einsum('btgp,gcp->btgc', xi, c_im.astype(f32)))
    return y, (xr[:, -1], xi[:, -1])


def s5_mixer(h, w_in, lam_re, lam_im, log_dt, b_re, b_im, c_re, c_im, d_skip, w_out, state):
    bsz, t_, _ = h.shape
    u = (h @ w_in).astype(jnp.float32)
    ug = u.reshape(bsz, t_, S5_GROUPS, S5_GC)
    x0r, x0i = state
    flip = lambda a: jnp.flip(a, axis=1)
    y_f, fin_f = s5_direction(ug, lam_re[0], lam_im[0], log_dt[0], b_re[0], b_im[0],
                              c_re[0], c_im[0], x0r[:, 0], x0i[:, 0])
    y_b, fin_b = s5_direction(flip(ug), lam_re[1], lam_im[1], log_dt[1], b_re[1], b_im[1],
                              c_re[1], c_im[1], x0r[:, 1], x0i[:, 1])
    y = (y_f + flip(y_b)).reshape(bsz, t_, S5_WIDTH) + d_skip.astype(jnp.float32) * u
    z = jax.nn.gelu(y).astype(h.dtype)
    a, gg = jnp.split(z @ w_out, 2, axis=-1)
    new_state = (jnp.stack([fin_f[0], fin_b[0]], axis=1).astype(h.dtype),
                 jnp.stack([fin_f[1], fin_b[1]], axis=1).astype(h.dtype))
    return a * jax.nn.sigmoid(gg), new_state


def layer_stack(x, cvec, st_c, st_n, st_m, st_re, st_im, rows,
                w_ada, b_ada, norm_g, final_g, w_ffn_in, w_ffn_out,
                m_w_in, m_conv_w, m_conv_b, m_gate_b, m_head_g, m_w_out,
                s5_w_in, s5_lam_re, s5_lam_im, s5_log_dt, s5_b_re, s5_b_im,
                s5_c_re, s5_c_im, s5_d, s5_w_out):
    new_c, new_n, new_m, new_re, new_im = [], [], [], [], []
    sc = jax.nn.silu(cvec)
    for l in range(DEPTH):
        mod = (sc @ w_ada[l] + b_ada[l]).reshape(-1, N_SUB, N_MOD, 1, D_MODEL)
        h = modulate(rms_norm(x, norm_g[l, 0]), mod[:, 0])
        x = x + 0.5 * mod[:, 0, 2] * swiglu(h, w_ffn_in[l, 0], w_ffn_out[l, 0])
        h = modulate(rms_norm(x, norm_g[l, 1]), mod[:, 1])
        j = l // N_MIXERS
        if l % N_MIXERS == 0:
            h, (cc, nn_, mm) = mlstm_mixer(h, m_w_in[j], m_conv_w[j], m_conv_b[j], m_gate_b[j],
                                           m_head_g[j], m_w_out[j],
                                           (st_c[:, j], st_n[:, j], st_m[:, j]), rows)
            new_c.append(cc)
            new_n.append(nn_)
            new_m.append(mm)
        else:
            h, (sr, si) = s5_mixer(h, s5_w_in[j], s5_lam_re[j], s5_lam_im[j], s5_log_dt[j],
                                   s5_b_re[j], s5_b_im[j], s5_c_re[j], s5_c_im[j], s5_d[j],
                                   s5_w_out[j], (st_re[:, j], st_im[:, j]))
            new_re.append(sr)
            new_im.append(si)
        x = x + mod[:, 1, 2] * h
        h = modulate(rms_norm(x, norm_g[l, 2]), mod[:, 2])
        x = x + 0.5 * mod[:, 2, 2] * swiglu(h, w_ffn_in[l, 1], w_ffn_out[l, 1])
    return (rms_norm(x, final_g), jnp.stack(new_c, axis=1), jnp.stack(new_n, axis=1),
            jnp.stack(new_m, axis=1), jnp.stack(new_re, axis=1), jnp.stack(new_im, axis=1))


def setup_inputs(seed: int = 0) -> dict:
    key = jax.random.key(seed)
    ks = iter(jax.random.split(key, 40))

    def nrm(shape, scale):
        return jax.random.normal(next(ks), shape, jnp.float32) * scale

    f_base = jnp.linspace(3.0, 6.0, M_HEADS)
    zb = jnp.zeros((M_HEADS,), jnp.float32)
    gate_base = jnp.stack([zb, f_base, zb, f_base])
    lam_im_base = jnp.pi * jnp.arange(S5_P, dtype=jnp.float32)
    s5_shape = (N_S5, 2, S5_GROUPS, S5_P)
    return {
        'x_prompt': nrm((BATCH, SEQ, D_MODEL), 1.0),
        'x_sample': nrm((DEC_BATCH, DEC_SEQ, D_MODEL), 1.0),
        'state_mlstm_C': nrm((DEC_BATCH, N_MLSTM, 2, M_HEADS, M_DK, M_DV), 0.1),
        'state_mlstm_n': jnp.abs(nrm((DEC_BATCH, N_MLSTM, 2, M_HEADS, M_DK), 0.5)),
        'state_mlstm_m': nrm((DEC_BATCH, N_MLSTM, 2, M_HEADS), 0.5),
        'state_s5_re': nrm((DEC_BATCH,) + s5_shape, 0.5),
        'state_s5_im': nrm((DEC_BATCH,) + s5_shape, 0.5),
        'c': nrm((DEC_BATCH, D_MODEL), 1.0),
        'c_ctx': nrm((D_MODEL,), 1.0),
        'w_ada': nrm((DEPTH, D_MODEL, N_SUB * N_MOD * D_MODEL), 0.5 * D_MODEL ** -0.5),
        'b_ada': nrm((DEPTH, N_SUB * N_MOD * D_MODEL), 0.02),
        'norm_g': 1.0 + nrm((DEPTH, N_SUB, D_MODEL), 0.02),
        'final_g': 1.0 + nrm((D_MODEL,), 0.02),
        'w_ffn_in': nrm((DEPTH, 2, D_MODEL, 2 * D_FF), D_MODEL ** -0.5),
        'w_ffn_out': nrm((DEPTH, 2, D_FF, D_MODEL), D_FF ** -0.5),
        'm_w_in': nrm((N_MLSTM, D_MODEL, M_PROJ), D_MODEL ** -0.5),
        'm_conv_w': nrm((N_MLSTM, M_CONV, M_CONV, 2 * M_QK), 1.0 / M_CONV),
        'm_conv_b': nrm((N_MLSTM, 2 * M_QK), 0.02),
        'm_gate_b': gate_base + nrm((N_MLSTM, M_GATES, M_HEADS), 0.1),
        'm_head_g': 1.0 + nrm((N_MLSTM, M_V), 0.02),
        'm_w_out': nrm((N_MLSTM, M_V, D_MODEL), M_V ** -0.5),
        's5_w_in': nrm((N_S5, D_MODEL, S5_WIDTH), D_MODEL ** -0.5),
        's5_lam_re': -0.5 + nrm(s5_shape, 0.01),
        's5_lam_im': lam_im_base + nrm(s5_shape, 0.01),
        's5_log_dt': jax.random.uniform(next(ks), (N_S5, 2, S5_GROUPS), jnp.float32,
                                        minval=math.log(DT_MIN), maxval=math.log(DT_MAX)),
        's5_b_re': nrm((N_S5, 2, S5_GROUPS, S5_P, S5_GC), (2 * S5_GC) ** -0.5),
        's5_b_im': nrm((N_S5, 2, S5_GROUPS, S5_P, S5_GC), (2 * S5_GC) ** -0.5),
        's5_c_re': nrm((N_S5, 2, S5_GROUPS, S5_GC, S5_P), S5_P ** -0.5),
        's5_c_im': nrm((N_S5, 2, S5_GROUPS, S5_GC, S5_P), S5_P ** -0.5),
        's5_d': nrm((N_S5, S5_WIDTH), 1.0),
        's5_w_out': nrm((N_S5, S5_WIDTH, 2 * D_MODEL), S5_WIDTH ** -0.5),
    }


def reference(x_prompt, x_sample, state_mlstm_C, state_mlstm_n, state_mlstm_m, state_s5_re,
              state_s5_im, c, c_ctx, w_ada, b_ada, norm_g, final_g, w_ffn_in, w_ffn_out,
              m_w_in, m_conv_w, m_conv_b, m_gate_b, m_head_g, m_w_out,
              s5_w_in, s5_lam_re, s5_lam_im, s5_log_dt, s5_b_re, s5_b_im,
              s5_c_re, s5_c_im, s5_d, s5_w_out):
    weights = (w_ada, b_ada, norm_g, final_g, w_ffn_in, w_ffn_out,
               m_w_in, m_conv_w, m_conv_b, m_gate_b, m_head_g, m_w_out,
               s5_w_in, s5_lam_re, s5_lam_im, s5_log_dt, s5_b_re, s5_b_im,
               s5_c_re, s5_c_im, s5_d, s5_w_out)
    bp = x_prompt.shape[0]
    dt_ = x_prompt.dtype
    z_c = jnp.zeros((bp, N_MLSTM, 2, M_HEADS, M_DK, M_DV), dt_)
    z_n = jnp.zeros((bp, N_MLSTM, 2, M_HEADS, M_DK), dt_)
    z_m = jnp.zeros((bp, N_MLSTM, 2, M_HEADS), dt_)
    z_s = jnp.zeros((bp, N_S5, 2, S5_GROUPS, S5_P), dt_)
    y_prompt, new_c, new_n, new_m, new_re, new_im = layer_stack(
        x_prompt, c_ctx[None, :], z_c, z_n, z_m, z_s, z_s, None, *weights)
    rows = x_sample.shape[1] // GRID_W
    y_sample = layer_stack(x_sample, c, state_mlstm_C, state_mlstm_n, state_mlstm_m,
                           state_s5_re, state_s5_im, rows, *weights)[0]
    return (y_prompt, y_sample, new_c, new_n, new_m, new_re, new_im)
```

```python
import functools
import math

import jax
import jax.numpy as jnp
from jax import lax
from jax.experimental import pallas as pl
from jax.experimental.pallas import tpu as pltpu

F32 = jnp.float32
BF16 = jnp.bfloat16

NORM_EPS = 1e-6
N_SUB = 3
N_MOD = 3
GRID_W = 64
M_HEADS = 8
M_GATES = 4
M_CHUNK = 64
S5_GC = 16
S5_P = 64
S5_SB_GROUPS = 16
S5_ROWS = 8

V7X_VMEM_LIMIT_BYTES = 60000 * 1024
LANES = 128
SUBLANES = 8


def _pick(n, prefs):
    for p in prefs:
        if p <= n and n % p == 0:
            return p
    return n


def _cparams(n_axes, est_bytes):
    limit = int(min(V7X_VMEM_LIMIT_BYTES, max(est_bytes, 16 * 1024 * 1024)))
    return pltpu.CompilerParams(dimension_semantics=("arbitrary",) * n_axes,
                                vmem_limit_bytes=limit)


def _nbytes(shape, dtype):
    return math.prod(shape) * jnp.dtype(dtype).itemsize


def _ada_body(n_cond, sb_ref, w_ref, b_ref, o_ref):
    k_dim, tn = w_ref.shape
    nl = tn // LANES

    def step(kc, acc):
        k0 = pl.multiple_of(kc * SUBLANES, SUBLANES)
        s = [sb_ref[c, pl.ds(k0, SUBLANES), :] for c in range(n_cond)]
        ws = [w_ref[pl.ds(k0, SUBLANES), l * LANES:(l + 1) * LANES] for l in range(nl)]
        return tuple(acc[c * nl + l] + ws[l] * s[c] for c in range(n_cond) for l in range(nl))

    acc0 = tuple(jnp.zeros((SUBLANES, LANES), F32) for _ in range(n_cond * nl))
    acc = lax.fori_loop(0, k_dim // SUBLANES, step, acc0, unroll=4)
    o_ref[...] = jnp.zeros(o_ref.shape, F32)
    for c in range(n_cond):
        for l in range(nl):
            o_ref[c:c + 1, l * LANES:(l + 1) * LANES] = (
                jnp.sum(acc[c * nl + l], axis=0, keepdims=True) + b_ref[:, l * LANES:(l + 1) * LANES])


def _ada(cond, w_ada, b_ada):
    n_cond, d = cond.shape
    depth, _, n_out = w_ada.shape
    sc = cond * jax.nn.sigmoid(cond)
    sb = jnp.broadcast_to(sc[:, :, None], (n_cond, d, LANES))
    tn = _pick(n_out, (512, 256, 128))
    est = 2 * (_nbytes((d, tn), F32) + _nbytes((n_cond, d, LANES), F32)) + (4 << 20)
    out = pl.pallas_call(
        functools.partial(_ada_body, n_cond),
        grid=(depth, n_out // tn),
        in_specs=[pl.BlockSpec((n_cond, d, LANES), lambda l, j: (0, 0, 0)),
                  pl.BlockSpec((None, d, tn), lambda l, j: (l, 0, j)),
                  pl.BlockSpec((None, 1, tn), lambda l, j: (l, 0, j))],
        out_specs=pl.BlockSpec((None, SUBLANES, tn), lambda l, j: (l, 0, j)),
        out_shape=jax.ShapeDtypeStruct((depth, SUBLANES, n_out), F32),
        compiler_params=_cparams(2, est),
        name="ada_mod",
    )(sb, w_ada, b_ada.reshape(depth, 1, n_out))
    return out[:, :n_cond]


def _norm_rows(x, g):
    ms = jnp.mean(x * x, axis=-1, keepdims=True)
    return x * lax.rsqrt(ms + NORM_EPS) * g


def _norm_mod_body(x_ref, g_ref, sh_ref, sc_ref, o_ref):
    y = _norm_rows(x_ref[...], g_ref[...])
    o_ref[...] = (y * (1.0 + sc_ref[...]) + sh_ref[...]).astype(o_ref.dtype)


def _norm_body(x_ref, g_ref, o_ref):
    o_ref[...] = _norm_rows(x_ref[...], g_ref[...]).astype(o_ref.dtype)


class _Rows:
    def __init__(self, n_ctx, n_lat_seq, lat_len):
        self.n_ctx = n_ctx
        self.lat_len = lat_len
        self.n = n_ctx + n_lat_seq * lat_len

    def cond_of_tile(self, i, tm):
        assert self.n_ctx % tm == 0 and self.lat_len % tm == 0
        pt = self.n_ctx // tm
        st = self.lat_len // tm
        return jnp.where(i < pt, 0, 1 + (i - pt) // st)


def _mod_spec(rows, tm, l, m, tn=None):
    def imap(i, *rest):
        j = rest[0] if (tn is not None and rest) else 0
        return (l, rows.cond_of_tile(i, tm), m, 0, j)
    return imap


def _norm_mod(x, g, mod, rows, l, sub, out_dtype=BF16):
    n, d = x.shape
    tm = _pick(math.gcd(rows.n_ctx, rows.lat_len), (256, 128, 64, 32, 16, 8))
    est = 2 * (_nbytes((tm, d), F32) + _nbytes((tm, d), out_dtype)) + 4 * _nbytes((tm, d), F32)
    return pl.pallas_call(
        _norm_mod_body,
        grid=(n // tm,),
        in_specs=[pl.BlockSpec((tm, d), lambda i: (i, 0)),
                  pl.BlockSpec((1, d), lambda i: (0, 0)),
                  pl.BlockSpec((None, None, None, 1, d), _mod_spec(rows, tm, l, sub * N_MOD + 0)),
                  pl.BlockSpec((None, None, None, 1, d), _mod_spec(rows, tm, l, sub * N_MOD + 1))],
        out_specs=pl.BlockSpec((tm, d), lambda i: (i, 0)),
        out_shape=jax.ShapeDtypeStruct((n, d), out_dtype),
        compiler_params=_cparams(1, est),
        name="norm_mod",
    )(x, g.reshape(1, d), mod, mod)


def _norm_mod_tm(x, g, mod, l, sub, row_off, n_seq, seq_len, n_pad, cond0):
    n, d = x.shape
    tt = _pick(seq_len, (256, 128, 64, 32, 16, 8))
    nt = seq_len // tt
    per_seq_cond = cond0 > 0

    def body(x_ref, g_ref, sh_ref, sc_ref, o_ref):
        b = pl.program_id(0)

        @pl.when(b < n_seq)
        def _():
            _norm_mod_body(x_ref, g_ref, sh_ref, sc_ref, o_ref)

        @pl.when(b >= n_seq)
        def _():
            o_ref[...] = jnp.zeros(o_ref.shape, o_ref.dtype)

    def x_map(b, t):
        return (row_off // tt + jnp.minimum(b, n_seq - 1) * nt + t, 0)

    def mod_map(m):
        def imap(b, t):
            c = cond0 + (jnp.minimum(b, n_seq - 1) if per_seq_cond else 0)
            return (l, c, m, 0, 0)
        return imap

    est = 2 * (_nbytes((tt, d), F32) + _nbytes((tt, d), BF16)) + 4 * _nbytes((tt, d), F32)
    return pl.pallas_call(
        body,
        grid=(n_pad, nt),
        in_specs=[pl.BlockSpec((tt, d), x_map),
                  pl.BlockSpec((1, d), lambda b, t: (0, 0)),
                  pl.BlockSpec((None, None, None, 1, d), mod_map(sub * N_MOD + 0)),
                  pl.BlockSpec((None, None, None, 1, d), mod_map(sub * N_MOD + 1))],
        out_specs=pl.BlockSpec((tt, d), lambda b, t: (t, b)),
        out_shape=jax.ShapeDtypeStruct((seq_len, n_pad * d), BF16),
        compiler_params=_cparams(2, est),
        name="norm_mod_tm",
    )(x, g.reshape(1, d), mod, mod)


def _final_norm(x, g, row_off, n_rows):
    n, d = x.shape
    tm = _pick(math.gcd(row_off, n_rows) if row_off else n_rows, (256, 128, 64, 32, 16, 8))
    est = 4 * _nbytes((tm, d), F32) + 4 * _nbytes((tm, d), F32)
    return pl.pallas_call(
        _norm_body,
        grid=(n_rows // tm,),
        in_specs=[pl.BlockSpec((tm, d), lambda i: (row_off // tm + i, 0)),
                  pl.BlockSpec((1, d), lambda i: (0, 0))],
        out_specs=pl.BlockSpec((tm, d), lambda i: (i, 0)),
        out_shape=jax.ShapeDtypeStruct((n_rows, d), F32),
        compiler_params=_cparams(1, est),
        name="final_norm",
    )(x, g.reshape(1, d))


def _dot(a, b):
    return jnp.dot(a, b, preferred_element_type=F32)


def _mm_plain_body(a_ref, w_ref, o_ref):
    o_ref[...] = _dot(a_ref[...], w_ref[...]).astype(o_ref.dtype)


def _mm_swiglu_body(a_ref, w_ref, o_ref):
    acc = _dot(a_ref[...], w_ref[...])
    half = acc.shape[1] // 2
    g = acc[:, :half]
    u = acc[:, half:]
    o_ref[...] = (g * jax.nn.sigmoid(g) * u).astype(o_ref.dtype)


def _mm_resid_body(coef, a_ref, w_ref, res_ref, gate_ref, o_ref):
    acc = _dot(a_ref[...], w_ref[...])
    o_ref[...] = res_ref[...] + (coef * gate_ref[...]) * acc


def _mm_glu_resid_body(a_ref, wa_ref, wg_ref, res_ref, gate_ref, o_ref):
    a = a_ref[...]
    ya = _dot(a, wa_ref[...])
    yg = _dot(a, wg_ref[...])
    o_ref[...] = res_ref[...] + gate_ref[...] * (ya * jax.nn.sigmoid(yg))


def _mm_tiles(m, k, n_out_tile_prefs, n):
    tm = _pick(m, (1024, 512, 256, 128, 64, 32, 16, 8))
    tn = _pick(n, n_out_tile_prefs)
    return tm, tn


def _mm_plain(a, w, out_dtype=F32, tn_prefs=(512, 256, 128)):
    m, k = a.shape
    n = w.shape[1]
    tm, tn = _mm_tiles(m, k, tn_prefs, n)
    est = (2 * (_nbytes((tm, k), a.dtype) + _nbytes((k, tn), w.dtype) + _nbytes((tm, tn), out_dtype))
           + 2 * _nbytes((tm, tn), F32) + (2 << 20))
    return pl.pallas_call(
        _mm_plain_body,
        grid=(m // tm, n // tn),
        in_specs=[pl.BlockSpec((tm, k), lambda i, j: (i, 0)),
                  pl.BlockSpec((k, tn), lambda i, j: (0, j))],
        out_specs=pl.BlockSpec((tm, tn), lambda i, j: (i, j)),
        out_shape=jax.ShapeDtypeStruct((m, n), out_dtype),
        compiler_params=_cparams(2, est),
        name="mm_plain",
    )(a, w)


def _mm_swiglu(h, w_il):
    m, k = h.shape
    f = w_il.shape[1] // 2
    tm = _pick(m, (1024, 512, 256, 128, 64, 32, 16, 8))
    tn = _swiglu_tile(f)
    est = (2 * (_nbytes((tm, k), BF16) + _nbytes((k, 2 * tn), BF16) + _nbytes((tm, tn), BF16))
           + 3 * _nbytes((tm, 2 * tn), F32) + (2 << 20))
    return pl.pallas_call(
        _mm_swiglu_body,
        grid=(m // tm, f // tn),
        in_specs=[pl.BlockSpec((tm, k), lambda i, j: (i, 0)),
                  pl.BlockSpec((k, 2 * tn), lambda i, j: (0, j))],
        out_specs=pl.BlockSpec((tm, tn), lambda i, j: (i, j)),
        out_shape=jax.ShapeDtypeStruct((m, f), BF16),
        compiler_params=_cparams(2, est),
        name="mm_swiglu",
    )(h, w_il)


def _swiglu_tile(f):
    return _pick(f, (256, 128))


def _interleave_swiglu(w_in):
    k, f2 = w_in.shape
    f = f2 // 2
    tn = _swiglu_tile(f)
    w = w_in.astype(BF16).reshape(k, 2, f // tn, tn)
    return jnp.transpose(w, (0, 2, 1, 3)).reshape(k, f2)


def _mm_resid(a, w, res, mod, rows, l, slot, coef, k_parts=1):
    m, k = a.shape
    n = w.shape[1]
    kp = k // k_parts
    tm = _pick(math.gcd(rows.n_ctx, rows.lat_len), (1024, 512, 256, 128, 64, 32, 16, 8))
    tn = _pick(n, (512, 256, 128))
    est = (2 * (_nbytes((tm, kp), BF16) + _nbytes((kp, tn), BF16) + 2 * _nbytes((tm, tn), F32))
           + 2 * _nbytes((tm, tn), F32) + (2 << 20))
    out = res
    for p in range(k_parts):
        out = pl.pallas_call(
            functools.partial(_mm_resid_body, coef),
            grid=(m // tm, n // tn),
            in_specs=[pl.BlockSpec((tm, kp), lambda i, j, p=p: (i, p)),
                      pl.BlockSpec((kp, tn), lambda i, j, p=p: (p, j)),
                      pl.BlockSpec((tm, tn), lambda i, j: (i, j)),
                      pl.BlockSpec((None, None, None, 1, tn), _mod_spec(rows, tm, l, slot, tn))],
            out_specs=pl.BlockSpec((tm, tn), lambda i, j: (i, j)),
            out_shape=jax.ShapeDtypeStruct((m, n), F32),
            compiler_params=_cparams(2, est),
            name="mm_resid",
        )(a, w, out, mod)
    return out


def _mm_glu_resid(a, w, res, mod, rows, l, slot):
    m, k = a.shape
    n = w.shape[1] // 2
    tm = _pick(math.gcd(rows.n_ctx, rows.lat_len), (1024, 512, 256, 128, 64, 32, 16, 8))
    tn = _pick(n, (256, 128))
    nj = n // tn
    est = (2 * (_nbytes((tm, k), BF16) + 2 * _nbytes((k, tn), BF16) + 2 * _nbytes((tm, tn), F32))
           + 4 * _nbytes((tm, tn), F32) + (2 << 20))
    return pl.pallas_call(
        _mm_glu_resid_body,
        grid=(m // tm, nj),
        in_specs=[pl.BlockSpec((tm, k), lambda i, j: (i, 0)),
                  pl.BlockSpec((k, tn), lambda i, j: (0, j)),
                  pl.BlockSpec((k, tn), lambda i, j: (0, nj + j)),
                  pl.BlockSpec((tm, tn), lambda i, j: (i, j)),
                  pl.BlockSpec((None, None, None, 1, tn), _mod_spec(rows, tm, l, slot, tn))],
        out_specs=pl.BlockSpec((tm, tn), lambda i, j: (i, j)),
        out_shape=jax.ShapeDtypeStruct((m, n), F32),
        compiler_params=_cparams(2, est),
        name="mm_glu_resid",
    )(a, w, w, res, mod)


def _conv_body(grid_rows, n_q_blocks, q_scale, x_ref, w_ref, b_ref, o_ref):
    x = x_ref[...]
    t_len = x.shape[0]
    t_idx = lax.broadcasted_iota(jnp.int32, x.shape, 0)
    if grid_rows is None:
        taps = [(0, dc) for dc in (-1, 0, 1)]
        col = t_idx
        n_cols = t_len
        row = jnp.zeros_like(t_idx)
        n_rows = 1
    else:
        taps = [(dr, dc) for dr in (-1, 0, 1) for dc in (-1, 0, 1)]
        col = t_idx % GRID_W
        n_cols = GRID_W
        row = t_idx // GRID_W
        n_rows = grid_rows
    acc = jnp.zeros(x.shape, F32)
    for dr, dc in taps:
        off = dr * n_cols + dc
        xs = x if off == 0 else pltpu.roll(x, shift=(-off) % t_len, axis=0)
        ok = ((col + dc >= 0) & (col + dc < n_cols) & (row + dr >= 0) & (row + dr < n_rows))
        wv = w_ref[dr + 1, dc + 1:dc + 2, :]
        acc = acc + jnp.where(ok, xs, 0.0) * wv
    y = acc + b_ref[...]
    y = y * jax.nn.sigmoid(y)
    scale = jnp.where(pl.program_id(1) < n_q_blocks, q_scale, 1.0)
    o_ref[...] = (y * scale).astype(o_ref.dtype)


def _qk_conv(proj, conv_w, conv_b, row_off, n_seq, seq_len, grid_rows, dk):
    n = proj.shape[0]
    ch = conv_w.shape[-1]
    tc = _pick(ch // 2, (512, 256, 128))
    n_q_blocks = (ch // 2) // tc
    body = functools.partial(_conv_body, grid_rows, n_q_blocks, float(dk) ** -0.5)
    est = 2 * (_nbytes((seq_len, tc), F32) + _nbytes((seq_len, tc), BF16)) + 8 * _nbytes((seq_len, tc), F32)
    return pl.pallas_call(
        body,
        grid=(n_seq, ch // tc),
        in_specs=[pl.BlockSpec((seq_len, tc), lambda b, c: (row_off // seq_len + b, c)),
                  pl.BlockSpec((3, 3, tc), lambda b, c: (0, 0, c)),
                  pl.BlockSpec((1, tc), lambda b, c: (0, c))],
        out_specs=pl.BlockSpec((seq_len, tc), lambda b, c: (b, c)),
        out_shape=jax.ShapeDtypeStruct((n_seq * seq_len, ch), BF16),
        compiler_params=_cparams(2, est),
        name="qk_conv",
    )(proj, conv_w, conv_b.reshape(1, ch))


def _log_sigmoid(x):
    return jnp.minimum(x, 0.0) - jnp.log1p(jnp.exp(-jnp.abs(x)))


def _mlstm_dir(backward, q, k, v, gc, gr, c_prev, n_prev, m_prev):
    l = q.shape[0]
    li_col = gc[:, 0:1]
    lf_col = _log_sigmoid(gc[:, 1:2])
    li_row = gr[0:1, :]
    lf_row = _log_sigmoid(gr[1:2, :])
    t_i = lax.broadcasted_iota(jnp.int32, (l, l), 0)
    s_i = lax.broadcasted_iota(jnp.int32, (l, l), 1)
    causal = (s_i >= t_i) if backward else (s_i <= t_i)
    causal_t = (t_i >= s_i) if backward else (t_i <= s_i)
    b_col = jnp.sum(jnp.where(causal, lf_row, 0.0), axis=1, keepdims=True)
    b_row = jnp.sum(jnp.where(causal_t, lf_col, 0.0), axis=0, keepdims=True)
    b_last = jnp.sum(lf_col, axis=0, keepdims=True)
    log_d = jnp.where(causal, b_col - b_row + li_row, -jnp.inf)
    log_inter = b_col + m_prev
    m_out = jnp.maximum(log_inter, jnp.max(log_d, axis=1, keepdims=True))
    dmat = jnp.exp(log_d - m_out)
    s_inter = jnp.exp(log_inter - m_out)
    s = lax.dot_general(q, k, (((1,), (1,)), ((), ())), preferred_element_type=F32) * dmat
    num = _dot(s.astype(BF16), v) + s_inter * _dot(q, c_prev.astype(BF16))
    den = (jnp.sum(s, axis=1, keepdims=True)
           + s_inter * jnp.sum(q.astype(F32) * n_prev, axis=1, keepdims=True))
    h = num / jnp.maximum(jnp.abs(den), jnp.exp(-m_out))
    log_w = b_last - b_col + li_col
    m_new = jnp.maximum(b_last + m_prev, jnp.max(log_w, axis=0, keepdims=True))
    w = jnp.exp(log_w - m_new)
    decay = jnp.exp(b_last + m_prev - m_new)
    kw = k.astype(F32) * w
    c_new = decay * c_prev + lax.dot_general(kw.astype(BF16), v, (((0,), (0,)), ((), ())),
                                             preferred_element_type=F32)
    n_new = decay * n_prev + jnp.sum(kw, axis=0, keepdims=True)
    return h, c_new, n_new, m_new


def _mlstm_body(zero_init, emit_state, *refs):
    it = iter(refs)
    qf, kf, vf, gcf, grf, qb, kb, vb, gcb, grb = (next(it) for _ in range(10))
    if not zero_init:
        c0, n0, m0 = next(it), next(it), next(it)
    hf_o, hb_o = next(it), next(it)
    if emit_state:
        c_o, n_o, m_o = next(it), next(it), next(it)
    c_s, n_s, m_s = next(it), next(it), next(it)
    c_idx = pl.program_id(2)

    @pl.when(c_idx == 0)
    def _():
        if zero_init:
            c_s[...] = jnp.zeros(c_s.shape, F32)
            n_s[...] = jnp.zeros(n_s.shape, F32)
            m_s[...] = jnp.zeros(m_s.shape, F32)
        else:
            c_s[...] = c0[...]
            n_s[...] = n0[...]
            m_s[...] = jnp.broadcast_to(m0[...], m_s.shape)

    ins = ((qf, kf, vf, gcf, grf, hf_o), (qb, kb, vb, gcb, grb, hb_o))
    for d, (q, k, v, gc, gr, h_o) in enumerate(ins):
        h, c_new, n_new, m_new = _mlstm_dir(
            d == 1, q[...], k[...], v[...].astype(BF16), gc[:, 2 * d:2 * d + 2], gr[2 * d:2 * d + 2, :],
            c_s[d], n_s[d], m_s[d][:, 0:1])
        h_o[...] = h
        c_s[d] = c_new
        n_s[d] = n_new
        m_s[d] = jnp.broadcast_to(m_new, m_s.shape[1:])

    if emit_state:
        @pl.when(c_idx == pl.num_programs(2) - 1)
        def _():
            c_o[...] = c_s[...]
            n_o[...] = n_s[...]
            m_o[...] = m_s[...]


def _mlstm_scan(qk, proj, gates, row_off, n_seq, seq_len, dk, dv, state):
    h_heads = M_HEADS
    lc = min(M_CHUNK, seq_len)
    nc = seq_len // lc
    n_rows = n_seq * seq_len
    zero_init = state is None
    g = lax.dynamic_slice_in_dim(gates, row_off, n_rows, axis=0).reshape(n_seq * nc, lc, M_GATES, h_heads)
    g_col = jnp.transpose(g, (0, 3, 1, 2))
    g_row = jnp.transpose(g, (0, 3, 2, 1))
    v_blk0 = (2 * h_heads * dk) // dv
    off_c = row_off // lc

    def fwd(b, h, c):
        return b * nc + c

    def bwd(b, h, c):
        return b * nc + (nc - 1 - c)

    def specs(cmap):
        return [pl.BlockSpec((lc, dk), lambda b, h, c: (cmap(b, h, c), h)),
                pl.BlockSpec((lc, dk), lambda b, h, c: (cmap(b, h, c), h_heads + h)),
                pl.BlockSpec((lc, dv), lambda b, h, c: (off_c + cmap(b, h, c), v_blk0 + h)),
                pl.BlockSpec((None, None, lc, M_GATES), lambda b, h, c: (cmap(b, h, c), h, 0, 0)),
                pl.BlockSpec((None, None, M_GATES, lc), lambda b, h, c: (cmap(b, h, c), h, 0, 0))]

    in_specs = specs(fwd) + specs(bwd)
    args = [qk, qk, proj, g_col, g_row] * 2
    if not zero_init:
        c0, n0, m0 = state
        in_specs += [pl.BlockSpec((None, 2, None, dk, dv), lambda b, h, c: (b, 0, h, 0, 0)),
                     pl.BlockSpec((None, 2, None, 1, dk), lambda b, h, c: (b, 0, h, 0, 0)),
                     pl.BlockSpec((None, 2, None, 1, 1), lambda b, h, c: (b, 0, h, 0, 0))]
        args += [c0, n0.reshape(n0.shape + (1,)).swapaxes(-1, -2), m0.reshape(m0.shape + (1, 1))]
    out_specs = [pl.BlockSpec((lc, dv), lambda b, h, c: (fwd(b, h, c), h)),
                 pl.BlockSpec((lc, dv), lambda b, h, c: (bwd(b, h, c), h))]
    out_shape = [jax.ShapeDtypeStruct((n_rows, h_heads * dv), F32)] * 2
    emit_state = zero_init
    if emit_state:
        out_specs += [pl.BlockSpec((None, 2, None, dk, dv), lambda b, h, c: (b, 0, h, 0, 0)),
                      pl.BlockSpec((None, 2, None, 1, dk), lambda b, h, c: (b, 0, h, 0, 0)),
                      pl.BlockSpec((None, 2, None, 1, LANES), lambda b, h, c: (b, 0, h, 0, 0))]
        out_shape += [jax.ShapeDtypeStruct((n_seq, 2, h_heads, dk, dv), F32),
                      jax.ShapeDtypeStruct((n_seq, 2, h_heads, 1, dk), F32),
                      jax.ShapeDtypeStruct((n_seq, 2, h_heads, 1, LANES), F32)]
    est = 6 * _nbytes((2, dk, dv), F32) + (24 << 20)
    outs = pl.pallas_call(
        functools.partial(_mlstm_body, zero_init, emit_state),
        grid=(n_seq, h_heads, nc),
        in_specs=in_specs,
        out_specs=out_specs,
        out_shape=out_shape,
        scratch_shapes=[pltpu.VMEM((2, dk, dv), F32), pltpu.VMEM((2, 1, dk), F32),
                        pltpu.VMEM((2, 1, LANES), F32)],
        compiler_params=_cparams(3, est),
        name="mlstm_scan",
    )(*args)
    h_f, h_b = outs[0], outs[1]
    fin = None
    if emit_state:
        fin = (outs[2], outs[3][:, :, :, 0, :], outs[4][:, :, :, 0, 0])
    return h_f, h_b, fin


def _mlstm_combine_body(hf_ref, hb_ref, o_ref, g_ref, z_ref):
    hh = _norm_rows(hf_ref[...] + hb_ref[...], g_ref[...])
    z_ref[...] = (hh * jax.nn.sigmoid(o_ref[...])).astype(z_ref.dtype)


def _mlstm_combine(h_f, h_b, proj, head_g, dk, dv, row_off):
    n_rows, width = h_f.shape
    tm = _pick(math.gcd(n_rows, row_off) if row_off else n_rows, (512, 256, 128, 64, 32, 16, 8))
    o_blk0 = (2 * M_HEADS * dk + M_HEADS * dv) // dv
    est = 2 * (3 * _nbytes((tm, dv), F32) + _nbytes((tm, dv), BF16)) + 4 * _nbytes((tm, dv), F32)
    return pl.pallas_call(
        _mlstm_combine_body,
        grid=(n_rows // tm, M_HEADS),
        in_specs=[pl.BlockSpec((tm, dv), lambda i, h: (i, h)),
                  pl.BlockSpec((tm, dv), lambda i, h: (i, h)),
                  pl.BlockSpec((tm, dv), lambda i, h: (row_off // tm + i, o_blk0 + h)),
                  pl.BlockSpec((1, dv), lambda i, h: (0, h))],
        out_specs=pl.BlockSpec((tm, dv), lambda i, h: (i, h)),
        out_shape=jax.ShapeDtypeStruct((n_rows, width), BF16),
        compiler_params=_cparams(2, est),
        name="mlstm_combine",
    )(h_f, h_b, proj, head_g.reshape(1, width))


def _s5_body(zero_init, emit_state, tt, *refs):
    it = iter(refs)
    u_ref, bw_ref, ar_ref, ai_ref, cw_ref = (next(it) for _ in range(5))
    if not zero_init:
        x0r_ref, x0i_ref = next(it), next(it)
    y_ref = next(it)
    if emit_state:
        fr_ref, fi_ref = next(it), next(it)
    e_s, xr_s, xi_s = next(it), next(it), next(it)
    d = pl.program_id(2)
    tc = pl.program_id(3)
    ns = xr_s.shape[1]

    @pl.when(tc == 0)
    def _():
        if zero_init:
            xr_s[...] = jnp.zeros(xr_s.shape, F32)
            xi_s[...] = jnp.zeros(xi_s.shape, F32)
        else:
            xr_s[...] = x0r_ref[...]
            xi_s[...] = x0i_ref[...]

    u2 = u_ref[...].reshape(tt * S5_ROWS, u_ref.shape[-1]).astype(BF16)
    e_s[...] = _dot(u2, bw_ref[...])
    ar = jnp.broadcast_to(ar_ref[...], (S5_ROWS, ns))
    ai = jnp.broadcast_to(ai_ref[...], (S5_ROWS, ns))

    def step(s, carry):
        xr, xi = carry
        t = s + d * (tt - 1 - 2 * s)
        r0 = pl.multiple_of(t * S5_ROWS, S5_ROWS)
        er = e_s[pl.ds(r0, S5_ROWS), 0:ns]
        ei = e_s[pl.ds(r0, S5_ROWS), ns:2 * ns]
        nxr = ar * xr - ai * xi + er
        nxi = ar * xi + ai * xr + ei
        e_s[pl.ds(r0, S5_ROWS), 0:ns] = nxr
        e_s[pl.ds(r0, S5_ROWS), ns:2 * ns] = nxi
        return nxr, nxi

    xr, xi = lax.fori_loop(0, tt, step, (xr_s[...], xi_s[...]), unroll=4)
    xr_s[...] = xr
    xi_s[...] = xi
    y = _dot(e_s[...].astype(BF16), cw_ref[...])
    y_ref[...] = y.reshape(y_ref.shape)

    if emit_state:
        @pl.when(tc == pl.num_programs(3) - 1)
        def _():
            fr_ref[...] = xr
            fi_ref[...] = xi


def _s5_scan(u_tm, bw, a_re, a_im, cw, seq_len, n_pad, state):
    e_width = u_tm.shape[-1]
    n_sb = bw.shape[1]
    cin = e_width // n_sb
    ns = bw.shape[-1] // 2
    tt = _pick(seq_len, (64, 32, 16, 8))
    nt = seq_len // tt
    n_bg = n_pad // S5_ROWS
    zero_init = state is None
    emit_state = zero_init

    def tmap(d, t):
        return t + d * (nt - 1 - 2 * t)

    in_specs = [pl.BlockSpec((tt, S5_ROWS, cin), lambda bg, sb, d, t: (tmap(d, t), bg, sb)),
                pl.BlockSpec((None, None, cin, 2 * ns), lambda bg, sb, d, t: (d, sb, 0, 0)),
                pl.BlockSpec((None, None, 1, ns), lambda bg, sb, d, t: (d, sb, 0, 0)),
                pl.BlockSpec((None, None, 1, ns), lambda bg, sb, d, t: (d, sb, 0, 0)),
                pl.BlockSpec((None, None, 2 * ns, cin), lambda bg, sb, d, t: (d, sb, 0, 0))]
    args = [u_tm, bw, a_re, a_im, cw]
    if not zero_init:
        in_specs += [pl.BlockSpec((None, S5_ROWS, ns), lambda bg, sb, d, t: (d, bg, sb))] * 2
        args += list(state)
    out_specs = [pl.BlockSpec((None, tt, S5_ROWS, cin), lambda bg, sb, d, t: (d, tmap(d, t), bg, sb))]
    out_shape = [jax.ShapeDtypeStruct((2, seq_len, n_pad, e_width), F32)]
    if emit_state:
        out_specs += [pl.BlockSpec((None, S5_ROWS, ns), lambda bg, sb, d, t: (d, bg, sb))] * 2
        out_shape += [jax.ShapeDtypeStruct((2, n_pad, n_sb * ns), F32)] * 2
    est = 3 * _nbytes((tt * S5_ROWS, 2 * ns), F32) + (12 << 20)
    outs = pl.pallas_call(
        functools.partial(_s5_body, zero_init, emit_state, tt),
        grid=(n_bg, n_sb, 2, nt),
        in_specs=in_specs,
        out_specs=out_specs,
        out_shape=out_shape,
        scratch_shapes=[pltpu.VMEM((tt * S5_ROWS, 2 * ns), F32),
                        pltpu.VMEM((S5_ROWS, ns), F32), pltpu.VMEM((S5_ROWS, ns), F32)],
        compiler_params=_cparams(4, est),
        name="s5_scan",
    )(*args)
    if emit_state:
        return outs[0], (outs[1], outs[2])
    return outs[0], None


def _s5_gelu_body(yf_ref, yb_ref, u_ref, d_ref, z_ref):
    y = yf_ref[...] + yb_ref[...] + d_ref[...] * u_ref[...]
    c = math.sqrt(2.0 / math.pi)
    z = 0.5 * y * (1.0 + jnp.tanh(c * (y + 0.044715 * (y * y * y))))
    z_ref[...] = z.astype(z_ref.dtype)


def _s5_gelu(y, u_tm2, d_skip, n_seq, seq_len, n_pad):
    e_width = d_skip.shape[-1]
    tt = _pick(seq_len, (256, 128, 64, 32, 16, 8))
    nt = seq_len // tt
    y2 = y.reshape(2, seq_len, n_pad * e_width)
    est = 2 * (3 * _nbytes((tt, e_width), F32) + _nbytes((tt, e_width), BF16)) + 6 * _nbytes((tt, e_width), F32)
    return pl.pallas_call(
        _s5_gelu_body,
        grid=(n_seq, nt),
        in_specs=[pl.BlockSpec((None, tt, e_width), lambda b, t: (0, t, b)),
                  pl.BlockSpec((None, tt, e_width), lambda b, t: (1, t, b)),
                  pl.BlockSpec((tt, e_width), lambda b, t: (t, b)),
                  pl.BlockSpec((1, e_width), lambda b, t: (0, 0))],
        out_specs=pl.BlockSpec((tt, e_width), lambda b, t: (b * nt + t, 0)),
        out_shape=jax.ShapeDtypeStruct((n_seq * seq_len, e_width), BF16),
        compiler_params=_cparams(2, est),
        name="s5_gelu",
    )(y2, y2, u_tm2, d_skip.reshape(1, e_width))


def _s5_weights(lam_re, lam_im, log_dt, b_re, b_im, c_re, c_im):
    lr = jnp.minimum(lam_re.astype(F32), -1e-4)
    li = lam_im.astype(F32)
    dt = jnp.exp(log_dt.astype(F32))[..., None]
    mag = jnp.exp(lr * dt)
    ar = mag * jnp.cos(li * dt)
    ai = mag * jnp.sin(li * dt)
    den = lr * lr + li * li
    xr = ar - 1.0
    cr = (xr * lr + ai * li) / den
    ci = (ai * lr - xr * li) / den
    bp_re = cr[..., None] * b_re - ci[..., None] * b_im
    bp_im = cr[..., None] * b_im + ci[..., None] * b_re
    n_dir, g, p = lam_re.shape
    gc = b_re.shape[-1]
    sbg = min(S5_SB_GROUPS, g)
    n_sb = g // sbg
    eye = jnp.eye(sbg, dtype=F32)

    def bd_in(bp):
        t = bp.reshape(n_dir, n_sb, sbg, p, gc)
        t = jnp.einsum('dsgpc,gh->dsgchp', t, eye)
        return t.reshape(n_dir, n_sb, sbg * gc, sbg * p)

    def bd_out(cc):
        t = cc.reshape(n_dir, n_sb, sbg, gc, p)
        t = jnp.einsum('dsgcp,gh->dsgphc', t, eye)
        return t.reshape(n_dir, n_sb, sbg * p, sbg * gc)

    bw = jnp.concatenate([bd_in(bp_re), bd_in(bp_im)], axis=-1).astype(BF16)
    cw = jnp.concatenate([bd_out(c_re.astype(F32)), -bd_out(c_im.astype(F32))], axis=-2).astype(BF16)
    a_re = ar.reshape(n_dir, n_sb, 1, sbg * p)
    a_im = ai.reshape(n_dir, n_sb, 1, sbg * p)
    return bw, a_re, a_im, cw


def kernel(x_prompt, x_sample, state_mlstm_C, state_mlstm_n, state_mlstm_m, state_s5_re, state_s5_im, c, c_ctx, w_ada, b_ada, norm_g, final_g, w_ffn_in, w_ffn_out, m_w_in, m_conv_w, m_conv_b, m_gate_b, m_head_g, m_w_out, s5_w_in, s5_lam_re, s5_lam_im, s5_log_dt, s5_b_re, s5_b_im, s5_c_re, s5_c_im, s5_d, s5_w_out):
    bp, tp, d = x_prompt.shape
    bs, ts, _ = x_sample.shape
    depth = w_ada.shape[0]
    n_ctx = bp * tp
    n_lat = bs * ts
    rows = _Rows(n_ctx, bs, ts)
    x = jnp.concatenate([x_prompt.reshape(n_ctx, d), x_sample.reshape(n_lat, d)], axis=0)

    cond = jnp.concatenate([c_ctx[None, :], c], axis=0).astype(F32)
    mod = _ada(cond, w_ada, b_ada).reshape(depth, 1 + bs, N_SUB * N_MOD, 1, d)

    dk = state_mlstm_C.shape[-2]
    dv = state_mlstm_C.shape[-1]
    qk_w = 2 * M_HEADS * dk
    main_w = qk_w + 2 * M_HEADS * dv
    g_groups, p_state = state_s5_re.shape[-2], state_s5_re.shape[-1]
    s_pad = -(-bs // S5_ROWS) * S5_ROWS

    new_c, new_n, new_m, new_re, new_im = [], [], [], [], []
    def ffn(x, l, ffn_idx, sub):
        h = _norm_mod(x, norm_g[l, sub], mod, rows, l, sub)
        a = _mm_swiglu(h, _interleave_swiglu(w_ffn_in[l, ffn_idx]))
        k_parts = 2 if a.shape[1] % (2 * LANES) == 0 and a.shape[1] > 4096 else 1
        return _mm_resid(a, w_ffn_out[l, ffn_idx].astype(BF16), x, mod, rows, l, sub * N_MOD + 2, 0.5,
                         k_parts=k_parts)

    for l in range(depth):
        x = ffn(x, l, 0, 0)
        for _ in (0,):
            if True:
                j = l // 2
                if l % 2 == 0:
                    h = _norm_mod(x, norm_g[l, 1], mod, rows, l, 1)
                    w_in = m_w_in[j].astype(BF16)
                    proj = _mm_plain(h, w_in[:, :main_w], F32, (1024, 512, 256, 128))
                    w_g = jnp.pad(w_in[:, main_w:], ((0, 0), (0, LANES - M_GATES * M_HEADS)))
                    gates = _mm_plain(h, w_g, F32, (LANES,))[:, :M_GATES * M_HEADS]
                    gates = gates.reshape(-1, M_GATES, M_HEADS) + m_gate_b[j].astype(F32)
                    qk_p = _qk_conv(proj, m_conv_w[j], m_conv_b[j], 0, bp, tp, None, dk)
                    qk_s = _qk_conv(proj, m_conv_w[j], m_conv_b[j], n_ctx, bs, ts, ts // GRID_W, dk)
                    hf_p, hb_p, fin = _mlstm_scan(qk_p, proj, gates, 0, bp, tp, dk, dv, None)
                    st = (state_mlstm_C[:, j], state_mlstm_n[:, j], state_mlstm_m[:, j])
                    hf_s, hb_s, _ = _mlstm_scan(qk_s, proj, gates, n_ctx, bs, ts, dk, dv, st)
                    z_p = _mlstm_combine(hf_p, hb_p, proj, m_head_g[j], dk, dv, 0)
                    z_s = _mlstm_combine(hf_s, hb_s, proj, m_head_g[j], dk, dv, n_ctx)
                    z = jnp.concatenate([z_p, z_s], axis=0)
                    x = _mm_resid(z, m_w_out[j].astype(BF16), x, mod, rows, l, 1 * N_MOD + 2, 1.0)
                    new_c.append(fin[0])
                    new_n.append(fin[1])
                    new_m.append(fin[2])
                else:
                    bw, a_re, a_im, cw = _s5_weights(s5_lam_re[j], s5_lam_im[j], s5_log_dt[j],
                                                     s5_b_re[j], s5_b_im[j], s5_c_re[j], s5_c_im[j])
                    w_in = s5_w_in[j].astype(BF16)
                    zs = []
                    for (row_off, n_seq, seq_len, n_pad, cond0, st) in (
                            (0, bp, tp, bp, 0, None),
                            (n_ctx, bs, ts, s_pad, 1,
                             (state_s5_re[:, j], state_s5_im[:, j]))):
                        h_tm = _norm_mod_tm(x, norm_g[l, 1], mod, l, 1, row_off, n_seq, seq_len, n_pad, cond0)
                        u_tm = _mm_plain(h_tm.reshape(seq_len * n_pad, d), w_in, F32, (512, 256, 128))
                        e_w = u_tm.shape[-1]
                        if st is not None:
                            st = tuple(jnp.pad(jnp.swapaxes(s, 0, 1).reshape(2, n_seq, g_groups * p_state),
                                               ((0, 0), (0, n_pad - n_seq), (0, 0))).astype(F32) for s in st)
                        y, fin = _s5_scan(u_tm.reshape(seq_len, n_pad, e_w), bw, a_re, a_im, cw,
                                          seq_len, n_pad, st)
                        zs.append(_s5_gelu(y, u_tm.reshape(seq_len, n_pad * e_w), s5_d[j], n_seq, seq_len, n_pad))
                        if fin is not None:
                            new_re.append(jnp.swapaxes(fin[0], 0, 1).reshape(n_seq, 2, g_groups, p_state))
                            new_im.append(jnp.swapaxes(fin[1], 0, 1).reshape(n_seq, 2, g_groups, p_state))
                    z = jnp.concatenate(zs, axis=0)
                    x = _mm_glu_resid(z, s5_w_out[j].astype(BF16), x, mod, rows, l, 1 * N_MOD + 2)
        x = ffn(x, l, 1, 2)

    y_prompt = _final_norm(x, final_g, 0, n_ctx).reshape(bp, tp, d)
    y_sample = _final_norm(x, final_g, n_ctx, n_lat).reshape(bs, ts, d)
    return (y_prompt, y_sample, jnp.stack(new_c, axis=1), jnp.stack(new_n, axis=1),
            jnp.stack(new_m, axis=1), jnp.stack(new_re, axis=1), jnp.stack(new_im, axis=1))
```

```python
import functools
import math

import jax
import jax.numpy as jnp
from jax import lax
from jax.experimental import pallas as pl
from jax.experimental.pallas import tpu as pltpu

F32 = jnp.float32
BF16 = jnp.bfloat16

NORM_EPS = 1e-6
N_SUB = 3
N_MOD = 3
GRID_W = 64
M_HEADS = 8
M_GATES = 4
M_CHUNK = 64
S5_SB_GROUPS = 16
S5_ROWS = 8
S5_SEG = 256

V7X_VMEM_LIMIT_BYTES = 60000 * 1024
LANES = 128
SUBLANES = 8
ROW_TILES = (1024, 512, 256, 128, 64, 32, 16, 8)


def _pick(n, prefs):
    for p in prefs:
        if p <= n and n % p == 0:
            return p
    return n


def _cparams(n_axes, est_bytes):
    limit = int(min(V7X_VMEM_LIMIT_BYTES, max(est_bytes, 16 * 1024 * 1024)))
    return pltpu.CompilerParams(dimension_semantics=("arbitrary",) * n_axes,
                                vmem_limit_bytes=limit)


def _nbytes(shape, dtype):
    return math.prod(shape) * jnp.dtype(dtype).itemsize


def _ada_body(n_cond, sb_ref, w_ref, b_ref, o_ref):
    k_dim, tn = w_ref.shape
    nl = tn // LANES

    def step(kc, acc):
        k0 = pl.multiple_of(kc * SUBLANES, SUBLANES)
        s = [sb_ref[c, pl.ds(k0, SUBLANES), :] for c in range(n_cond)]
        ws = [w_ref[pl.ds(k0, SUBLANES), l * LANES:(l + 1) * LANES] for l in range(nl)]
        return tuple(acc[c * nl + l] + ws[l] * s[c] for c in range(n_cond) for l in range(nl))

    acc0 = tuple(jnp.zeros((SUBLANES, LANES), F32) for _ in range(n_cond * nl))
    acc = lax.fori_loop(0, k_dim // SUBLANES, step, acc0, unroll=4)
    o_ref[...] = jnp.zeros(o_ref.shape, F32)
    for c in range(n_cond):
        for l in range(nl):
            o_ref[c:c + 1, l * LANES:(l + 1) * LANES] = (
                jnp.sum(acc[c * nl + l], axis=0, keepdims=True) + b_ref[:, l * LANES:(l + 1) * LANES])


def _ada(cond, w_ada, b_ada):
    n_cond, d = cond.shape
    depth, _, n_out = w_ada.shape
    sc = cond * jax.nn.sigmoid(cond)
    sb = jnp.broadcast_to(sc[:, :, None], (n_cond, d, LANES))
    tn = _pick(n_out, (512, 256, 128))
    est = 2 * (_nbytes((d, tn), F32) + _nbytes((n_cond, d, LANES), F32)) + (4 << 20)
    out = pl.pallas_call(
        functools.partial(_ada_body, n_cond),
        grid=(depth, n_out // tn),
        in_specs=[pl.BlockSpec((n_cond, d, LANES), lambda l, j: (0, 0, 0)),
                  pl.BlockSpec((None, d, tn), lambda l, j: (l, 0, j)),
                  pl.BlockSpec((None, 1, tn), lambda l, j: (l, 0, j))],
        out_specs=pl.BlockSpec((None, SUBLANES, tn), lambda l, j: (l, 0, j)),
        out_shape=jax.ShapeDtypeStruct((depth, SUBLANES, n_out), F32),
        compiler_params=_cparams(2, est),
        name="ada_mod",
    )(sb, w_ada, b_ada.reshape(depth, 1, n_out))
    return out[:, :n_cond]


def _norm_rows(x, g):
    ms = jnp.mean(x * x, axis=-1, keepdims=True)
    return x * lax.rsqrt(ms + NORM_EPS) * g


def _norm_mod_body(x_ref, g_ref, sh_ref, sc_ref, o_ref):
    y = _norm_rows(x_ref[...], g_ref[...])
    o_ref[...] = (y * (1.0 + sc_ref[...]) + sh_ref[...]).astype(o_ref.dtype)


def _norm_body(x_ref, g_ref, o_ref):
    o_ref[...] = _norm_rows(x_ref[...], g_ref[...]).astype(o_ref.dtype)


class _Rows:
    def __init__(self, n_ctx, n_lat_seq, lat_len):
        self.n_ctx = n_ctx
        self.lat_len = lat_len
        self.n = n_ctx + n_lat_seq * lat_len

    def tile(self, prefs=ROW_TILES):
        return _pick(math.gcd(self.n_ctx, self.lat_len), prefs)

    def cond_of_tile(self, i, tm):
        assert self.n_ctx % tm == 0 and self.lat_len % tm == 0
        pt = self.n_ctx // tm
        st = self.lat_len // tm
        return jnp.where(i < pt, 0, 1 + (i - pt) // st)


def _mod_spec(rows, tm, l, slot, tn):
    def imap(i, *rest):
        return (l, rows.cond_of_tile(i, tm), slot, 0, rest[0] if rest else 0)
    return pl.BlockSpec((None, None, None, 1, tn), imap)


def _norm_mod(x, g, mod, rows, l, sub):
    n, d = x.shape
    tm = rows.tile((256, 128, 64, 32, 16, 8))
    est = 2 * (_nbytes((tm, d), F32) + _nbytes((tm, d), BF16)) + 4 * _nbytes((tm, d), F32)
    return pl.pallas_call(
        _norm_mod_body,
        grid=(n // tm,),
        in_specs=[pl.BlockSpec((tm, d), lambda i: (i, 0)),
                  pl.BlockSpec((1, d), lambda i: (0, 0)),
                  _mod_spec(rows, tm, l, sub * N_MOD + 0, d),
                  _mod_spec(rows, tm, l, sub * N_MOD + 1, d)],
        out_specs=pl.BlockSpec((tm, d), lambda i: (i, 0)),
        out_shape=jax.ShapeDtypeStruct((n, d), BF16),
        compiler_params=_cparams(1, est),
        name="norm_mod",
    )(x, g.reshape(1, d), mod, mod)


def _norm_mod_tm(x, g, mod, l, sub, row_off, n_seq, seq_len, n_pad, cond_of_seq):
    n, d = x.shape
    tt = _pick(seq_len, (256, 128, 64, 32, 16, 8))
    nt = seq_len // tt

    def body(x_ref, g_ref, sh_ref, sc_ref, o_ref):
        b = pl.program_id(0)

        @pl.when(b < n_seq)
        def _():
            _norm_mod_body(x_ref, g_ref, sh_ref, sc_ref, o_ref)

        @pl.when(b >= n_seq)
        def _():
            o_ref[...] = jnp.zeros(o_ref.shape, o_ref.dtype)

    def x_map(b, t):
        return (row_off // tt + jnp.minimum(b, n_seq - 1) * nt + t, 0)

    def mod_spec(slot):
        return pl.BlockSpec((None, None, None, 1, d),
                            lambda b, t: (l, cond_of_seq(jnp.minimum(b, n_seq - 1)), slot, 0, 0))

    est = 2 * (_nbytes((tt, d), F32) + _nbytes((tt, d), BF16)) + 4 * _nbytes((tt, d), F32)
    return pl.pallas_call(
        body,
        grid=(n_pad, nt),
        in_specs=[pl.BlockSpec((tt, d), x_map),
                  pl.BlockSpec((1, d), lambda b, t: (0, 0)),
                  mod_spec(sub * N_MOD + 0),
                  mod_spec(sub * N_MOD + 1)],
        out_specs=pl.BlockSpec((tt, d), lambda b, t: (t, b)),
        out_shape=jax.ShapeDtypeStruct((seq_len, n_pad * d), BF16),
        compiler_params=_cparams(2, est),
        name="norm_mod_tm",
    )(x, g.reshape(1, d), mod, mod)


def _final_norm(x, g, row_off, n_rows):
    n, d = x.shape
    tm = _pick(math.gcd(row_off, n_rows) if row_off else n_rows, (256, 128, 64, 32, 16, 8))
    est = 8 * _nbytes((tm, d), F32)
    return pl.pallas_call(
        _norm_body,
        grid=(n_rows // tm,),
        in_specs=[pl.BlockSpec((tm, d), lambda i: (row_off // tm + i, 0)),
                  pl.BlockSpec((1, d), lambda i: (0, 0))],
        out_specs=pl.BlockSpec((tm, d), lambda i: (i, 0)),
        out_shape=jax.ShapeDtypeStruct((n_rows, d), F32),
        compiler_params=_cparams(1, est),
        name="final_norm",
    )(x, g.reshape(1, d))


def _dot(a, b):
    return jnp.dot(a, b, preferred_element_type=F32)


def _mm_body(kind, coef, n_w, *refs):
    a_ref, w_refs, rest = refs[0], refs[1:1 + n_w], refs[1 + n_w:]
    ws = [w[...].astype(BF16) for w in w_refs]
    acc = _dot(a_ref[...], ws[0] if n_w == 1 else jnp.concatenate(ws, axis=1))
    tn = w_refs[0].shape[1]
    if kind == "plain":
        (o_ref,) = rest
        o_ref[...] = acc.astype(o_ref.dtype)
    elif kind == "swiglu":
        (o_ref,) = rest
        g, u = acc[:, :tn], acc[:, tn:]
        o_ref[...] = (g * jax.nn.sigmoid(g) * u).astype(o_ref.dtype)
    elif kind == "resid":
        res_ref, gate_ref, o_ref = rest
        o_ref[...] = res_ref[...] + (coef * gate_ref[...]) * acc
    elif kind == "glu_resid":
        res_ref, gate_ref, o_ref = rest
        o_ref[...] = res_ref[...] + gate_ref[...] * (acc[:, :tn] * jax.nn.sigmoid(acc[:, tn:]))
    else:
        raise ValueError(kind)


def _mm(kind, a, w, lead, col_blocks, tn, n_out, out_dtype, *, k_part=(0, 1), coef=1.0,
        res=None, gate=None, tm=None):
    m = a.shape[0]
    kp = w.shape[-2] // k_part[1]
    n_w = len(col_blocks)
    if tm is None:
        tm = _pick(m, ROW_TILES)
    n_lead = len(lead)
    in_specs = [pl.BlockSpec((tm, kp), lambda i, j: (i, k_part[0]))]
    in_specs += [pl.BlockSpec((None,) * n_lead + (kp, tn),
                              lambda i, j, cb=cb: tuple(lead) + (k_part[0], cb(j)))
                 for cb in col_blocks]
    args = [a] + [w] * n_w
    if res is not None:
        mod, rows, l, slot = gate
        in_specs += [pl.BlockSpec((tm, tn), lambda i, j: (i, j)), _mod_spec(rows, tm, l, slot, tn)]
        args += [res, mod]
    est = (2 * (_nbytes((tm, kp), BF16) + n_w * _nbytes((kp, tn), F32) + _nbytes((tm, tn), out_dtype))
           + n_w * _nbytes((kp, tn), BF16) + 3 * n_w * _nbytes((tm, tn), F32)
           + (4 * _nbytes((tm, tn), F32) if res is not None else 0) + (2 << 20))
    return pl.pallas_call(
        functools.partial(_mm_body, kind, coef, n_w),
        grid=(m // tm, n_out // tn),
        in_specs=in_specs,
        out_specs=pl.BlockSpec((tm, tn), lambda i, j: (i, j)),
        out_shape=jax.ShapeDtypeStruct((m, n_out), out_dtype),
        compiler_params=_cparams(2, est),
        name="mm_" + kind,
    )(*args)


def _conv_body(grid_rows, n_q_blocks, q_scale, x_ref, w_ref, b_ref, o_ref):
    x = x_ref[...]
    t_len = x.shape[0]
    t_idx = lax.broadcasted_iota(jnp.int32, x.shape, 0)
    if grid_rows is None:
        taps = [(0, dc) for dc in (-1, 0, 1)]
        col = t_idx
        n_cols = t_len
        row = jnp.zeros_like(t_idx)
        n_rows = 1
    else:
        taps = [(dr, dc) for dr in (-1, 0, 1) for dc in (-1, 0, 1)]
        col = t_idx % GRID_W
        n_cols = GRID_W
        row = t_idx // GRID_W
        n_rows = grid_rows
    acc = jnp.zeros(x.shape, F32)
    for dr, dc in taps:
        off = dr * n_cols + dc
        xs = x if off == 0 else pltpu.roll(x, shift=(-off) % t_len, axis=0)
        ok = ((col + dc >= 0) & (col + dc < n_cols) & (row + dr >= 0) & (row + dr < n_rows))
        wv = w_ref[dr + 1, dc + 1:dc + 2, :]
        acc = acc + jnp.where(ok, xs, 0.0) * wv
    y = acc + b_ref[...]
    y = y * jax.nn.sigmoid(y)
    scale = jnp.where(pl.program_id(1) < n_q_blocks, q_scale, 1.0)
    o_ref[...] = (y * scale).astype(o_ref.dtype)


def _qk_conv(proj, conv_w, conv_b, row_off, n_seq, seq_len, grid_rows, dk):
    ch = conv_w.shape[-1]
    tc = _pick(ch // 2, (512, 256, 128))
    n_q_blocks = (ch // 2) // tc
    body = functools.partial(_conv_body, grid_rows, n_q_blocks, float(dk) ** -0.5)
    est = 2 * (_nbytes((seq_len, tc), F32) + _nbytes((seq_len, tc), BF16)) + 8 * _nbytes((seq_len, tc), F32)
    return pl.pallas_call(
        body,
        grid=(n_seq, ch // tc),
        in_specs=[pl.BlockSpec((seq_len, tc), lambda b, c: (row_off // seq_len + b, c)),
                  pl.BlockSpec((3, 3, tc), lambda b, c: (0, 0, c)),
                  pl.BlockSpec((1, tc), lambda b, c: (0, c))],
        out_specs=pl.BlockSpec((seq_len, tc), lambda b, c: (b, c)),
        out_shape=jax.ShapeDtypeStruct((n_seq * seq_len, ch), BF16),
        compiler_params=_cparams(2, est),
        name="qk_conv",
    )(proj, conv_w, conv_b.reshape(1, ch))


def _log_sigmoid(x):
    return jnp.minimum(x, 0.0) - jnp.log1p(jnp.exp(-jnp.abs(x)))


def _mlstm_dir(backward, q, k, v, li_col, fp_col, li_row, fp_row, c_prev, n_prev, m_prev):
    l = q.shape[0]
    lf_col = _log_sigmoid(fp_col)
    lf_row = _log_sigmoid(fp_row)
    t_i = lax.broadcasted_iota(jnp.int32, (l, l), 0)
    s_i = lax.broadcasted_iota(jnp.int32, (l, l), 1)
    causal = (s_i >= t_i) if backward else (s_i <= t_i)
    causal_t = (t_i >= s_i) if backward else (t_i <= s_i)
    b_col = jnp.sum(jnp.where(causal, lf_row, 0.0), axis=1, keepdims=True)
    b_row = jnp.sum(jnp.where(causal_t, lf_col, 0.0), axis=0, keepdims=True)
    b_last = jnp.sum(lf_col, axis=0, keepdims=True)
    log_d = jnp.where(causal, b_col - b_row + li_row, -jnp.inf)
    log_inter = b_col + m_prev
    m_out = jnp.maximum(log_inter, jnp.max(log_d, axis=1, keepdims=True))
    dmat = jnp.exp(log_d - m_out)
    s_inter = jnp.exp(log_inter - m_out)
    s = lax.dot_general(q, k, (((1,), (1,)), ((), ())), preferred_element_type=F32) * dmat
    num = _dot(s.astype(BF16), v) + s_inter * _dot(q, c_prev.astype(BF16))
    den = (jnp.sum(s, axis=1, keepdims=True)
           + s_inter * jnp.sum(q.astype(F32) * n_prev, axis=1, keepdims=True))
    h = num / jnp.maximum(jnp.abs(den), jnp.exp(-m_out))
    log_w = b_last - b_col + li_col
    m_new = jnp.maximum(b_last + m_prev, jnp.max(log_w, axis=0, keepdims=True))
    w = jnp.exp(log_w - m_new)
    decay = jnp.exp(b_last + m_prev - m_new)
    kw = k.astype(F32) * w
    c_new = decay * c_prev + lax.dot_general(kw.astype(BF16), v, (((0,), (0,)), ((), ())),
                                             preferred_element_type=F32)
    n_new = decay * n_prev + jnp.sum(kw, axis=0, keepdims=True)
    return h, c_new, n_new, m_new


def _mlstm_body(zero_init, emit_state, n_heads, dk, dv, *refs):
    it = iter(refs)
    fwd_in = tuple(next(it) for _ in range(5))
    bwd_in = tuple(next(it) for _ in range(5))
    if not zero_init:
        c0, n0, m0 = next(it), next(it), next(it)
    if emit_state:
        next(it), next(it), next(it)
    hf_o, hb_o = next(it), next(it)
    if emit_state:
        c_o, n_o, m_o = next(it), next(it), next(it)
    c_s, n_s, m_s = next(it), next(it), next(it)
    c_idx = pl.program_id(1)

    @pl.when(c_idx == 0)
    def _():
        if zero_init:
            c_s[...] = jnp.zeros(c_s.shape, F32)
            n_s[...] = jnp.zeros(n_s.shape, F32)
            m_s[...] = jnp.zeros(m_s.shape, F32)
        else:
            c_s[...] = c0[...]
            n_s[...] = n0[...]
            m_s[...] = m0[...]

    for d, ((q_ref, k_ref, v_ref, gc_ref, gr_ref), h_o) in enumerate(((fwd_in, hf_o), (bwd_in, hb_o))):
        for h in range(n_heads):
            gi, gf = (2 * d) * n_heads + h, (2 * d + 1) * n_heads + h
            hh, c_new, n_new, m_new = _mlstm_dir(
                d == 1,
                q_ref[:, h * dk:(h + 1) * dk], k_ref[:, h * dk:(h + 1) * dk],
                v_ref[:, h * dv:(h + 1) * dv].astype(BF16),
                gc_ref[:, gi:gi + 1], gc_ref[:, gf:gf + 1], gr_ref[gi:gi + 1, :], gr_ref[gf:gf + 1, :],
                c_s[d, h], n_s[d, h:h + 1, :], m_s[d:d + 1, h:h + 1])
            h_o[:, h * dv:(h + 1) * dv] = hh
            c_s[d, h] = c_new
            n_s[d, h:h + 1, :] = n_new
            m_s[d:d + 1, h:h + 1] = m_new

    if emit_state:
        @pl.when(c_idx == pl.num_programs(1) - 1)
        def _():
            c_o[...] = c_s[...]
            n_o[...] = n_s[...]
            m_o[...] = m_s[...]


def _mlstm_scan(qk, proj, gates, row_off, n_seq, seq_len, dk, dv, layer_j, state_in=None, state_out=None):
    nh = M_HEADS
    lc = min(M_CHUNK, seq_len)
    nc = seq_len // lc
    n_rows = n_seq * seq_len
    zero_init = state_in is None
    emit_state = state_out is not None
    n_tot = gates.shape[0]
    g_col = gates.reshape(n_tot // lc, lc, M_GATES * nh)
    g_row = jnp.swapaxes(g_col, 1, 2)
    off_c = row_off // lc
    k_blk = 1
    v_blk = (2 * nh * dk) // (nh * dv)

    def specs(cmap):
        return [pl.BlockSpec((lc, nh * dk), lambda b, c: (cmap(b, c), 0)),
                pl.BlockSpec((lc, nh * dk), lambda b, c: (cmap(b, c), k_blk)),
                pl.BlockSpec((lc, nh * dv), lambda b, c: (off_c + cmap(b, c), v_blk)),
                pl.BlockSpec((None, lc, M_GATES * nh), lambda b, c: (off_c + cmap(b, c), 0, 0)),
                pl.BlockSpec((None, M_GATES * nh, lc), lambda b, c: (off_c + cmap(b, c), 0, 0))]

    def fwd(b, c):
        return b * nc + c

    def bwd(b, c):
        return b * nc + (nc - 1 - c)

    def state_specs():
        return [pl.BlockSpec((None, None, 2, nh, dk, dv), lambda b, c: (b, layer_j, 0, 0, 0, 0)),
                pl.BlockSpec((None, None, 2, nh, dk), lambda b, c: (b, layer_j, 0, 0, 0)),
                pl.BlockSpec((None, None, 2, nh), lambda b, c: (b, layer_j, 0, 0))]

    in_specs = specs(fwd) + specs(bwd)
    args = [qk, qk, proj, g_col, g_row] * 2
    if not zero_init:
        in_specs += state_specs()
        args += list(state_in)
    aliases = {}
    if emit_state:
        for t, s in enumerate(state_out):
            aliases[len(args)] = 2 + t
            in_specs.append(pl.BlockSpec(memory_space=pl.ANY))
            args.append(s)
    out_specs = [pl.BlockSpec((lc, nh * dv), lambda b, c: (fwd(b, c), 0)),
                 pl.BlockSpec((lc, nh * dv), lambda b, c: (bwd(b, c), 0))]
    out_shape = [jax.ShapeDtypeStruct((n_rows, nh * dv), F32)] * 2
    if emit_state:
        out_specs += state_specs()
        out_shape += [jax.ShapeDtypeStruct(s.shape, F32) for s in state_out]
    state_bytes = _nbytes((2, nh, dk, dv), F32)
    est = state_bytes * (1 + (2 if not zero_init else 0) + (2 if emit_state else 0)) + (16 << 20)
    outs = pl.pallas_call(
        functools.partial(_mlstm_body, zero_init, emit_state, nh, dk, dv),
        grid=(n_seq, nc),
        in_specs=in_specs,
        out_specs=out_specs,
        out_shape=out_shape,
        scratch_shapes=[pltpu.VMEM((2, nh, dk, dv), F32), pltpu.VMEM((2, nh, dk), F32),
                        pltpu.VMEM((2, nh), F32)],
        input_output_aliases=aliases,
        compiler_params=_cparams(2, est),
        name="mlstm_scan",
    )(*args)
    return outs[0], outs[1], (tuple(outs[2:5]) if emit_state else None)


def _mlstm_combine_body(hf_ref, hb_ref, o_ref, g_ref, z_ref):
    hh = _norm_rows(hf_ref[...] + hb_ref[...], g_ref[...])
    z_ref[...] = (hh * jax.nn.sigmoid(o_ref[...])).astype(z_ref.dtype)


def _mlstm_combine(h_f, h_b, proj, head_g, dk, dv, row_off):
    n_rows, width = h_f.shape
    tm = _pick(math.gcd(n_rows, row_off) if row_off else n_rows, (512, 256, 128, 64, 32, 16, 8))
    o_blk0 = (2 * M_HEADS * dk + M_HEADS * dv) // dv
    est = 2 * (3 * _nbytes((tm, dv), F32) + _nbytes((tm, dv), BF16)) + 4 * _nbytes((tm, dv), F32)
    return pl.pallas_call(
        _mlstm_combine_body,
        grid=(n_rows // tm, M_HEADS),
        in_specs=[pl.BlockSpec((tm, dv), lambda i, h: (i, h)),
                  pl.BlockSpec((tm, dv), lambda i, h: (i, h)),
                  pl.BlockSpec((tm, dv), lambda i, h: (row_off // tm + i, o_blk0 + h)),
                  pl.BlockSpec((1, dv), lambda i, h: (0, h))],
        out_specs=pl.BlockSpec((tm, dv), lambda i, h: (i, h)),
        out_shape=jax.ShapeDtypeStruct((n_rows, width), BF16),
        compiler_params=_cparams(2, est),
        name="mlstm_combine",
    )(h_f, h_b, proj, head_g.reshape(1, width))


def _s5_body(zero_init, emit_y, emit_state, tt, *refs):
    it = iter(refs)
    u_ref, bw_ref, ar_ref, ai_ref = (next(it) for _ in range(4))
    cw_ref = next(it) if emit_y else None
    if not zero_init:
        x0r_ref, x0i_ref = next(it), next(it)
    y_ref = next(it) if emit_y else None
    if emit_state:
        fr_ref, fi_ref = next(it), next(it)
    e_s, xr_s, xi_s = next(it), next(it), next(it)
    d = pl.program_id(2)
    tc = pl.program_id(3)
    ns = xr_s.shape[1]

    @pl.when(tc == 0)
    def _():
        if zero_init:
            xr_s[...] = jnp.zeros(xr_s.shape, F32)
            xi_s[...] = jnp.zeros(xi_s.shape, F32)
        else:
            xr_s[...] = x0r_ref[...]
            xi_s[...] = x0i_ref[...]

    u2 = u_ref[...].reshape(tt * S5_ROWS, u_ref.shape[-1]).astype(BF16)
    e_s[...] = _dot(u2, bw_ref[...])
    ar = jnp.broadcast_to(ar_ref[...], (S5_ROWS, ns))
    ai = jnp.broadcast_to(ai_ref[...], (S5_ROWS, ns))

    def step(s, carry):
        xr, xi = carry
        t = s + d * (tt - 1 - 2 * s)
        r0 = pl.multiple_of(t * S5_ROWS, S5_ROWS)
        er = e_s[pl.ds(r0, S5_ROWS), 0:ns]
        ei = e_s[pl.ds(r0, S5_ROWS), ns:2 * ns]
        nxr = ar * xr - ai * xi + er
        nxi = ar * xi + ai * xr + ei
        if emit_y:
            e_s[pl.ds(r0, S5_ROWS), 0:ns] = nxr
            e_s[pl.ds(r0, S5_ROWS), ns:2 * ns] = nxi
        return nxr, nxi

    xr, xi = lax.fori_loop(0, tt, step, (xr_s[...], xi_s[...]), unroll=8)
    xr_s[...] = xr
    xi_s[...] = xi
    if emit_y:
        y = _dot(e_s[...].astype(BF16), cw_ref[...])
        y_ref[...] = y.reshape(y_ref.shape)

    if emit_state:
        @pl.when(tc == pl.num_programs(3) - 1)
        def _():
            fr_ref[...] = xr
            fi_ref[...] = xi


def _s5_scan(u_tm, bw, a_re, a_im, cw, state, emit_y, emit_state):
    seq_len, n_pad, e_width = u_tm.shape
    n_sb = bw.shape[1]
    cin = e_width // n_sb
    ns = bw.shape[-1] // 2
    tt = _pick(seq_len, (256, 128, 64, 32, 16, 8))
    nt = seq_len // tt
    n_bg = n_pad // S5_ROWS
    zero_init = state is None

    def tmap(d, t):
        return t + d * (nt - 1 - 2 * t)

    def wspec(r, c):
        return pl.BlockSpec((None, None, r, c), lambda bg, sb, d, t: (d, sb, 0, 0))

    def sspec():
        return pl.BlockSpec((None, S5_ROWS, ns), lambda bg, sb, d, t: (d, bg, sb))

    in_specs = [pl.BlockSpec((tt, S5_ROWS, cin), lambda bg, sb, d, t: (tmap(d, t), bg, sb)),
                wspec(cin, 2 * ns), wspec(1, ns), wspec(1, ns)]
    args = [u_tm, bw, a_re, a_im]
    if emit_y:
        in_specs.append(wspec(2 * ns, cin))
        args.append(cw)
    if not zero_init:
        in_specs += [sspec(), sspec()]
        args += list(state)
    out_specs, out_shape = [], []
    if emit_y:
        out_specs.append(pl.BlockSpec((None, tt, S5_ROWS, cin), lambda bg, sb, d, t: (d, tmap(d, t), bg, sb)))
        out_shape.append(jax.ShapeDtypeStruct((2, seq_len, n_pad, e_width), F32))
    if emit_state:
        out_specs += [sspec(), sspec()]
        out_shape += [jax.ShapeDtypeStruct((2, n_pad, n_sb * ns), F32)] * 2
    est = (3 * _nbytes((tt * S5_ROWS, 2 * ns), F32) + 6 * _nbytes((tt, S5_ROWS, cin), F32)
           + 4 * _nbytes((cin, 2 * ns), BF16) + (4 << 20))
    outs = pl.pallas_call(
        functools.partial(_s5_body, zero_init, emit_y, emit_state, tt),
        grid=(n_bg, n_sb, 2, nt),
        in_specs=in_specs,
        out_specs=out_specs,
        out_shape=out_shape,
        scratch_shapes=[pltpu.VMEM((tt * S5_ROWS, 2 * ns), F32),
                        pltpu.VMEM((S5_ROWS, ns), F32), pltpu.VMEM((S5_ROWS, ns), F32)],
        compiler_params=_cparams(4, est),
        name="s5_scan",
    )(*args)
    y = outs[0] if emit_y else None
    fin = tuple(outs[-2:]) if emit_state else None
    return y, fin


def _s5_carry_body(n_seq, n_seg, fr_ref, fi_ref, x0r_ref, x0i_ref, pr_ref, pi_ref, or_ref, oi_ref):
    or_ref[...] = jnp.zeros(or_ref.shape, F32)
    oi_ref[...] = jnp.zeros(oi_ref.shape, F32)
    for d in range(2):
        pr, pi = pr_ref[d], pi_ref[d]
        order = range(n_seg) if d == 0 else range(n_seg - 1, -1, -1)
        for b in range(n_seq):
            cr, ci = x0r_ref[d, b:b + 1, :], x0i_ref[d, b:b + 1, :]
            for k in order:
                r = b * n_seg + k
                or_ref[d, r:r + 1, :] = cr
                oi_ref[d, r:r + 1, :] = ci
                fr, fi = fr_ref[d, r:r + 1, :], fi_ref[d, r:r + 1, :]
                cr, ci = pr * cr - pi * ci + fr, pr * ci + pi * cr + fi


def _s5_carry(fin, x0, a_pow, n_seq, n_seg):
    shape = fin[0].shape
    return pl.pallas_call(
        functools.partial(_s5_carry_body, n_seq, n_seg),
        out_shape=[jax.ShapeDtypeStruct(shape, F32)] * 2,
        compiler_params=_cparams(0, 16 * _nbytes(shape, F32)),
        name="s5_carry",
    )(fin[0], fin[1], x0[0], x0[1], a_pow[0], a_pow[1])


def _s5_gelu_body(yf_ref, yb_ref, u_ref, d_ref, z_ref):
    c = math.sqrt(2.0 / math.pi)
    for b in range(S5_ROWS):
        y = yf_ref[:, b, :] + yb_ref[:, b, :] + d_ref[...] * u_ref[:, b, :]
        z = 0.5 * y * (1.0 + jnp.tanh(c * (y + 0.044715 * (y * y * y))))
        z_ref[b] = z.astype(z_ref.dtype)


def _s5_gelu(y, u_tm, d_skip):
    seq_len, n_pad, e_width = u_tm.shape
    tt = _pick(seq_len, (128, 64, 32, 16, 8))
    ec = _pick(e_width, (512, 256, 128))
    blk = (tt, S5_ROWS, ec)
    est = 2 * (3 * _nbytes(blk, F32) + _nbytes(blk, BF16)) + 8 * _nbytes(blk, F32)
    return pl.pallas_call(
        _s5_gelu_body,
        grid=(n_pad // S5_ROWS, seq_len // tt, e_width // ec),
        in_specs=[pl.BlockSpec((None,) + blk, lambda bg, t, e: (0, t, bg, e)),
                  pl.BlockSpec((None,) + blk, lambda bg, t, e: (1, t, bg, e)),
                  pl.BlockSpec(blk, lambda bg, t, e: (t, bg, e)),
                  pl.BlockSpec((1, ec), lambda bg, t, e: (0, e))],
        out_specs=pl.BlockSpec((S5_ROWS, tt, ec), lambda bg, t, e: (bg, t, e)),
        out_shape=jax.ShapeDtypeStruct((n_pad, seq_len, e_width), BF16),
        compiler_params=_cparams(3, est),
        name="s5_gelu",
    )(y, y, u_tm, d_skip.reshape(1, e_width))


def _s5_weights(lam_re, lam_im, log_dt, b_re, b_im, c_re, c_im, seg_len):
    lr = jnp.minimum(lam_re.astype(F32), -1e-4)
    li = lam_im.astype(F32)
    dt = jnp.exp(log_dt.astype(F32))[..., None]
    mag = jnp.exp(lr * dt)
    ar = mag * jnp.cos(li * dt)
    ai = mag * jnp.sin(li * dt)
    den = lr * lr + li * li
    xr = ar - 1.0
    cr = (xr * lr + ai * li) / den
    ci = (ai * lr - xr * li) / den
    bp_re = cr[..., None] * b_re - ci[..., None] * b_im
    bp_im = cr[..., None] * b_im + ci[..., None] * b_re
    n_dir, g, p = lam_re.shape
    gc = b_re.shape[-1]
    sbg = min(S5_SB_GROUPS, g)
    n_sb = g // sbg
    eye = jnp.eye(sbg, dtype=F32)

    def bd_in(bp):
        t = bp.reshape(n_dir, n_sb, sbg, p, gc)
        t = jnp.einsum('dsgpc,gh->dsgchp', t, eye)
        return t.reshape(n_dir, n_sb, sbg * gc, sbg * p)

    def bd_out(cc):
        t = cc.reshape(n_dir, n_sb, sbg, gc, p)
        t = jnp.einsum('dsgcp,gh->dsgphc', t, eye)
        return t.reshape(n_dir, n_sb, sbg * p, sbg * gc)

    bw = jnp.concatenate([bd_in(bp_re), bd_in(bp_im)], axis=-1).astype(BF16)
    cw = jnp.concatenate([bd_out(c_re.astype(F32)), -bd_out(c_im.astype(F32))], axis=-2).astype(BF16)
    a_re = ar.reshape(n_dir, n_sb, 1, sbg * p)
    a_im = ai.reshape(n_dir, n_sb, 1, sbg * p)
    pr, pi = jnp.ones_like(ar), jnp.zeros_like(ai)
    br, bi = ar, ai
    e = seg_len
    while e:
        if e & 1:
            pr, pi = pr * br - pi * bi, pr * bi + pi * br
        br, bi = br * br - bi * bi, 2.0 * br * bi
        e >>= 1
    a_pow = (pr.reshape(n_dir, 1, g * p), pi.reshape(n_dir, 1, g * p))
    return bw, a_re, a_im, cw, a_pow


def _s5_group(x, g, mod, l, w_in, lead, weights, d_skip, row_off, n_seq, seq_len, cond0, per_seq_cond,
              state, want_state):
    bw, a_re, a_im, cw, a_pow = weights
    d = x.shape[1]
    seg = _pick(seq_len, (S5_SEG, 128, 64, 32, 16, 8))
    n_seg = seq_len // seg
    n_ps = n_seq * n_seg
    n_pad = -(-n_ps // S5_ROWS) * S5_ROWS

    def cond_of(p):
        return cond0 + (p // n_seg if per_seq_cond else 0)

    h_tm = _norm_mod_tm(x, g, mod, l, 1, row_off, n_ps, seg, n_pad, cond_of)
    e_w = w_in.shape[-1]
    u_tm = _mm("plain", h_tm.reshape(seg * n_pad, d), w_in, lead, [lambda j: j],
               _pick(e_w, (512, 256, 128)), e_w, F32).reshape(seg, n_pad, e_w)

    def pad_rows(s):
        return jnp.pad(s, ((0, 0), (0, n_pad - s.shape[1]), (0, 0)))

    if n_seg == 1:
        start = None if state is None else tuple(pad_rows(s) for s in state)
    else:
        assert not want_state
        _, fin = _s5_scan(u_tm, bw, a_re, a_im, cw, None, emit_y=False, emit_state=True)
        x0 = state if state is not None else tuple(jnp.zeros((2, n_seq, fin[0].shape[-1]), F32) for _ in range(2))
        start = _s5_carry(fin, x0, a_pow, n_seq, n_seg)
    y, fin = _s5_scan(u_tm, bw, a_re, a_im, cw, start, emit_y=True, emit_state=want_state)
    z = _s5_gelu(y, u_tm, d_skip).reshape(n_pad * seg, e_w)[:n_seq * seq_len]
    if want_state:
        fin = tuple(f[:, :n_seq] for f in fin)
    return z, fin


def kernel(x_prompt, x_sample, state_mlstm_C, state_mlstm_n, state_mlstm_m, state_s5_re, state_s5_im, c, c_ctx, w_ada, b_ada, norm_g, final_g, w_ffn_in, w_ffn_out, m_w_in, m_conv_w, m_conv_b, m_gate_b, m_head_g, m_w_out, s5_w_in, s5_lam_re, s5_lam_im, s5_log_dt, s5_b_re, s5_b_im, s5_c_re, s5_c_im, s5_d, s5_w_out):
    bp, tp, d = x_prompt.shape
    bs, ts, _ = x_sample.shape
    depth = w_ada.shape[0]
    n_ctx, n_lat = bp * tp, bs * ts
    rows = _Rows(n_ctx, bs, ts)
    tm = rows.tile()
    x = jnp.concatenate([x_prompt.reshape(n_ctx, d), x_sample.reshape(n_lat, d)], axis=0)

    cond = jnp.concatenate([c_ctx[None, :], c], axis=0).astype(F32)
    mod = _ada(cond, w_ada, b_ada).reshape(depth, 1 + bs, N_SUB * N_MOD, 1, d)

    n_ml, n_s5 = m_w_in.shape[0], s5_w_in.shape[0]
    dk, dv = state_mlstm_C.shape[-2], state_mlstm_C.shape[-1]
    qk_w = 2 * M_HEADS * dk
    main_w = qk_w + 2 * M_HEADS * dv
    g_groups, p_state = state_s5_re.shape[-2], state_s5_re.shape[-1]
    d_ff = w_ffn_out.shape[-2]

    def gate_of(l, sub):
        return (mod, rows, l, sub * N_MOD + 2)

    def ffn(x, l, idx, sub):
        h = _norm_mod(x, norm_g[l, sub], mod, rows, l, sub)
        tn = _pick(d_ff, (256, 128))
        nf = d_ff // tn
        a = _mm("swiglu", h, w_ffn_in, (l, idx), [lambda j: j, lambda j: nf + j], tn, d_ff, BF16)
        k_parts = 2 if d_ff % (2 * LANES) == 0 and d_ff > 4096 else 1
        for p in range(k_parts):
            x = _mm("resid", a, w_ffn_out, (l, idx), [lambda j: j], _pick(d, (256, 128)), d, F32,
                    k_part=(p, k_parts), coef=0.5, res=x, gate=gate_of(l, sub), tm=tm)
        return x

    def mlstm(x, l, j, new_state):
        h = _norm_mod(x, norm_g[l, 1], mod, rows, l, 1)
        proj = _mm("plain", h, m_w_in, (j,), [lambda jb: jb], _pick(main_w, (512, 256, 128)), main_w, F32)
        w_g = jnp.pad(m_w_in[j, :, main_w:], ((0, 0), (0, LANES - M_GATES * M_HEADS)))
        gates = _mm("plain", h, w_g, (), [lambda jb: jb], LANES, LANES, F32)[:, :M_GATES * M_HEADS]
        gates = gates + m_gate_b[j].reshape(1, M_GATES * M_HEADS).astype(F32)
        qk_p = _qk_conv(proj, m_conv_w[j], m_conv_b[j], 0, bp, tp, None, dk)
        qk_s = _qk_conv(proj, m_conv_w[j], m_conv_b[j], n_ctx, bs, ts, ts // GRID_W, dk)
        hf_p, hb_p, new_state = _mlstm_scan(qk_p, proj, gates, 0, bp, tp, dk, dv, j, None, new_state)
        cached = (state_mlstm_C, state_mlstm_n, state_mlstm_m)
        hf_s, hb_s, _ = _mlstm_scan(qk_s, proj, gates, n_ctx, bs, ts, dk, dv, j, cached, None)
        z = jnp.concatenate([_mlstm_combine(hf_p, hb_p, proj, m_head_g[j], dk, dv, 0),
                             _mlstm_combine(hf_s, hb_s, proj, m_head_g[j], dk, dv, n_ctx)], axis=0)
        x = _mm("resid", z, m_w_out, (j,), [lambda jb: jb], _pick(d, (512, 256, 128)), d, F32,
                res=x, gate=gate_of(l, 1), tm=tm)
        return x, new_state

    def s5(x, l, j):
        seg = _pick(ts, (S5_SEG, 128, 64, 32, 16, 8))
        weights = _s5_weights(s5_lam_re[j], s5_lam_im[j], s5_log_dt[j], s5_b_re[j], s5_b_im[j],
                              s5_c_re[j], s5_c_im[j], seg)
        cached = tuple(jnp.swapaxes(s[:, j], 0, 1).reshape(2, bs, g_groups * p_state).astype(F32)
                       for s in (state_s5_re, state_s5_im))
        z_p, fin = _s5_group(x, norm_g[l, 1], mod, l, s5_w_in, (j,), weights, s5_d[j],
                             0, bp, tp, 0, False, None, True)
        z_s, _ = _s5_group(x, norm_g[l, 1], mod, l, s5_w_in, (j,), weights, s5_d[j],
                           n_ctx, bs, ts, 1, True, cached, False)
        z = jnp.concatenate([z_p, z_s], axis=0)
        tn = _pick(d, (256, 128))
        nd = d // tn
        x = _mm("glu_resid", z, s5_w_out, (j,), [lambda jb: jb, lambda jb: nd + jb], tn, d, F32,
                res=x, gate=gate_of(l, 1), tm=tm)
        fin = tuple(jnp.swapaxes(f, 0, 1).reshape(bp, 2, g_groups, p_state) for f in fin)
        return x, fin

    new_ml = (jnp.zeros((bp, n_ml, 2, M_HEADS, dk, dv), F32), jnp.zeros((bp, n_ml, 2, M_HEADS, dk), F32),
              jnp.zeros((bp, n_ml, 2, M_HEADS), F32))
    new_re, new_im = [], []
    for l in range(depth):
        x = ffn(x, l, 0, 0)
        if l % 2 == 0:
            x, new_ml = mlstm(x, l, l // 2, new_ml)
        else:
            x, (f_re, f_im) = s5(x, l, l // 2)
            new_re.append(f_re)
            new_im.append(f_im)
        x = ffn(x, l, 1, 2)

    y_prompt = _final_norm(x, final_g, 0, n_ctx).reshape(bp, tp, d)
    y_sample = _final_norm(x, final_g, n_ctx, n_lat).reshape(bs, ts, d)
    return (y_prompt, y_sample) + tuple(new_ml) + (jnp.stack(new_re, axis=1), jnp.stack(new_im, axis=1))
```

```python
import functools
import math

import jax
import jax.numpy as jnp
from jax import lax
from jax.experimental import pallas as pl
from jax.experimental.pallas import tpu as pltpu

F32 = jnp.float32
BF16 = jnp.bfloat16

NORM_EPS = 1e-6
N_SUB = 3
N_MOD = 3
GRID_W = 64
M_HEADS = 8
M_GATES = 4
M_SCAN_CHUNK = 256
M_HEADS_PER_STEP = 4
S5_SB_GROUPS = 16
S5_ROWS = 8
S5_SEG = 256

V7X_VMEM_LIMIT_BYTES = 60000 * 1024
LANES = 128
SUBLANES = 8
ROW_TILES = (1024, 512, 256, 128, 64, 32, 16, 8)


def _pick(n, prefs):
    for p in prefs:
        if p <= n and n % p == 0:
            return p
    return n


def _cparams(n_axes, est_bytes):
    limit = int(min(V7X_VMEM_LIMIT_BYTES, max(est_bytes, 16 * 1024 * 1024)))
    return pltpu.CompilerParams(dimension_semantics=("arbitrary",) * n_axes,
                                vmem_limit_bytes=limit)


def _nbytes(shape, dtype):
    return math.prod(shape) * jnp.dtype(dtype).itemsize


def _ada_body(n_cond, sb_ref, w_ref, b_ref, o_ref):
    k_dim, tn = w_ref.shape
    nl = tn // LANES

    def step(kc, acc):
        k0 = pl.multiple_of(kc * SUBLANES, SUBLANES)
        s = [sb_ref[c, pl.ds(k0, SUBLANES), :] for c in range(n_cond)]
        ws = [w_ref[pl.ds(k0, SUBLANES), l * LANES:(l + 1) * LANES] for l in range(nl)]
        return tuple(acc[c * nl + l] + ws[l] * s[c] for c in range(n_cond) for l in range(nl))

    acc0 = tuple(jnp.zeros((SUBLANES, LANES), F32) for _ in range(n_cond * nl))
    acc = lax.fori_loop(0, k_dim // SUBLANES, step, acc0, unroll=4)
    o_ref[...] = jnp.zeros(o_ref.shape, F32)
    for c in range(n_cond):
        for l in range(nl):
            o_ref[c:c + 1, l * LANES:(l + 1) * LANES] = (
                jnp.sum(acc[c * nl + l], axis=0, keepdims=True) + b_ref[:, l * LANES:(l + 1) * LANES])


def _ada(cond, w_ada, b_ada):
    n_cond, d = cond.shape
    depth, _, n_out = w_ada.shape
    sc = cond * jax.nn.sigmoid(cond)
    sb = jnp.broadcast_to(sc[:, :, None], (n_cond, d, LANES))
    tn = _pick(n_out, (512, 256, 128))
    est = 2 * (_nbytes((d, tn), F32) + _nbytes((n_cond, d, LANES), F32)) + (4 << 20)
    out = pl.pallas_call(
        functools.partial(_ada_body, n_cond),
        grid=(depth, n_out // tn),
        in_specs=[pl.BlockSpec((n_cond, d, LANES), lambda l, j: (0, 0, 0)),
                  pl.BlockSpec((None, d, tn), lambda l, j: (l, 0, j)),
                  pl.BlockSpec((None, 1, tn), lambda l, j: (l, 0, j))],
        out_specs=pl.BlockSpec((None, SUBLANES, tn), lambda l, j: (l, 0, j)),
        out_shape=jax.ShapeDtypeStruct((depth, SUBLANES, n_out), F32),
        compiler_params=_cparams(2, est),
        name="ada_mod",
    )(sb, w_ada, b_ada.reshape(depth, 1, n_out))
    return out[:, :n_cond]


def _norm_rows(x, g):
    ms = jnp.mean(x * x, axis=-1, keepdims=True)
    return x * lax.rsqrt(ms + NORM_EPS) * g


def _norm_mod_body(x_ref, g_ref, sh_ref, sc_ref, o_ref):
    y = _norm_rows(x_ref[...], g_ref[...])
    o_ref[...] = (y * (1.0 + sc_ref[...]) + sh_ref[...]).astype(o_ref.dtype)


def _norm_body(x_ref, g_ref, o_ref):
    o_ref[...] = _norm_rows(x_ref[...], g_ref[...]).astype(o_ref.dtype)


class _Rows:
    def __init__(self, n_ctx, n_lat_seq, lat_len):
        self.n_ctx = n_ctx
        self.lat_len = lat_len
        self.n = n_ctx + n_lat_seq * lat_len

    def tile(self, prefs=ROW_TILES):
        return _pick(math.gcd(self.n_ctx, self.lat_len), prefs)

    def cond_of_tile(self, i, tm):
        assert self.n_ctx % tm == 0 and self.lat_len % tm == 0
        pt = self.n_ctx // tm
        st = self.lat_len // tm
        return jnp.where(i < pt, 0, 1 + (i - pt) // st)


def _mod_spec(rows, tm, l, slot, tn):
    def imap(i, *rest):
        return (l, rows.cond_of_tile(i, tm), slot, 0, rest[0] if rest else 0)
    return pl.BlockSpec((None, None, None, 1, tn), imap)


def _norm_mod(x, g, mod, rows, l, sub):
    n, d = x.shape
    tm = rows.tile((256, 128, 64, 32, 16, 8))
    est = 2 * (_nbytes((tm, d), F32) + _nbytes((tm, d), BF16)) + 4 * _nbytes((tm, d), F32)
    return pl.pallas_call(
        _norm_mod_body,
        grid=(n // tm,),
        in_specs=[pl.BlockSpec((tm, d), lambda i: (i, 0)),
                  pl.BlockSpec((1, d), lambda i: (0, 0)),
                  _mod_spec(rows, tm, l, sub * N_MOD + 0, d),
                  _mod_spec(rows, tm, l, sub * N_MOD + 1, d)],
        out_specs=pl.BlockSpec((tm, d), lambda i: (i, 0)),
        out_shape=jax.ShapeDtypeStruct((n, d), BF16),
        compiler_params=_cparams(1, est),
        name="norm_mod",
    )(x, g.reshape(1, d), mod, mod)


def _norm_mod_tm(x, g, mod, l, sub, row_off, n_seq, seq_len, n_pad, cond_of_seq):
    n, d = x.shape
    tt = _pick(seq_len, (256, 128, 64, 32, 16, 8))
    nt = seq_len // tt

    def body(x_ref, g_ref, sh_ref, sc_ref, o_ref):
        b = pl.program_id(0)

        @pl.when(b < n_seq)
        def _():
            _norm_mod_body(x_ref, g_ref, sh_ref, sc_ref, o_ref)

        @pl.when(b >= n_seq)
        def _():
            o_ref[...] = jnp.zeros(o_ref.shape, o_ref.dtype)

    def x_map(b, t):
        return (row_off // tt + jnp.minimum(b, n_seq - 1) * nt + t, 0)

    def mod_spec(slot):
        return pl.BlockSpec((None, None, None, 1, d),
                            lambda b, t: (l, cond_of_seq(jnp.minimum(b, n_seq - 1)), slot, 0, 0))

    est = 2 * (_nbytes((tt, d), F32) + _nbytes((tt, d), BF16)) + 4 * _nbytes((tt, d), F32)
    return pl.pallas_call(
        body,
        grid=(n_pad, nt),
        in_specs=[pl.BlockSpec((tt, d), x_map),
                  pl.BlockSpec((1, d), lambda b, t: (0, 0)),
                  mod_spec(sub * N_MOD + 0),
                  mod_spec(sub * N_MOD + 1)],
        out_specs=pl.BlockSpec((tt, d), lambda b, t: (t, b)),
        out_shape=jax.ShapeDtypeStruct((seq_len, n_pad * d), BF16),
        compiler_params=_cparams(2, est),
        name="norm_mod_tm",
    )(x, g.reshape(1, d), mod, mod)


def _final_norm(x, g, row_off, n_rows):
    n, d = x.shape
    tm = _pick(math.gcd(row_off, n_rows) if row_off else n_rows, (256, 128, 64, 32, 16, 8))
    est = 8 * _nbytes((tm, d), F32)
    return pl.pallas_call(
        _norm_body,
        grid=(n_rows // tm,),
        in_specs=[pl.BlockSpec((tm, d), lambda i: (row_off // tm + i, 0)),
                  pl.BlockSpec((1, d), lambda i: (0, 0))],
        out_specs=pl.BlockSpec((tm, d), lambda i: (i, 0)),
        out_shape=jax.ShapeDtypeStruct((n_rows, d), F32),
        compiler_params=_cparams(1, est),
        name="final_norm",
    )(x, g.reshape(1, d))


def _dot(a, b):
    return jnp.dot(a, b, preferred_element_type=F32)


def _mm_body(kind, coef, n_w, *refs):
    a_ref, w_refs, rest = refs[0], refs[1:1 + n_w], refs[1 + n_w:]
    ws = [w[...].astype(BF16) for w in w_refs]
    acc = _dot(a_ref[...], ws[0] if n_w == 1 else jnp.concatenate(ws, axis=1))
    tn = w_refs[0].shape[1]
    if kind == "plain":
        (o_ref,) = rest
        o_ref[...] = acc.astype(o_ref.dtype)
    elif kind == "swiglu":
        (o_ref,) = rest
        g, u = acc[:, :tn], acc[:, tn:]
        o_ref[...] = (g * jax.nn.sigmoid(g) * u).astype(o_ref.dtype)
    elif kind == "resid":
        res_ref, gate_ref, o_ref = rest
        o_ref[...] = res_ref[...] + (coef * gate_ref[...]) * acc
    elif kind == "glu_resid":
        res_ref, gate_ref, o_ref = rest
        o_ref[...] = res_ref[...] + gate_ref[...] * (acc[:, :tn] * jax.nn.sigmoid(acc[:, tn:]))
    else:
        raise ValueError(kind)


def _mm(kind, a, w, lead, col_blocks, tn, n_out, out_dtype, *, k_part=(0, 1), coef=1.0,
        res=None, gate=None, tm=None):
    m = a.shape[0]
    kp = w.shape[-2] // k_part[1]
    n_w = len(col_blocks)
    if tm is None:
        tm = _pick(m, ROW_TILES)
    n_lead = len(lead)
    in_specs = [pl.BlockSpec((tm, kp), lambda i, j: (i, k_part[0]))]
    in_specs += [pl.BlockSpec((None,) * n_lead + (kp, tn),
                              lambda i, j, cb=cb: tuple(lead) + (k_part[0], cb(j)))
                 for cb in col_blocks]
    args = [a] + [w] * n_w
    if res is not None:
        mod, rows, l, slot = gate
        in_specs += [pl.BlockSpec((tm, tn), lambda i, j: (i, j)), _mod_spec(rows, tm, l, slot, tn)]
        args += [res, mod]
    est = (2 * (_nbytes((tm, kp), BF16) + n_w * _nbytes((kp, tn), F32) + _nbytes((tm, tn), out_dtype))
           + n_w * _nbytes((kp, tn), BF16) + 3 * n_w * _nbytes((tm, tn), F32)
           + (4 * _nbytes((tm, tn), F32) if res is not None else 0) + (2 << 20))
    return pl.pallas_call(
        functools.partial(_mm_body, kind, coef, n_w),
        grid=(m // tm, n_out // tn),
        in_specs=in_specs,
        out_specs=pl.BlockSpec((tm, tn), lambda i, j: (i, j)),
        out_shape=jax.ShapeDtypeStruct((m, n_out), out_dtype),
        compiler_params=_cparams(2, est),
        name="mm_" + kind,
    )(*args)


def _conv_body(grid_rows, n_q_blocks, q_scale, x_ref, w_ref, b_ref, o_ref):
    x = x_ref[...]
    t_len = x.shape[0]
    t_idx = lax.broadcasted_iota(jnp.int32, x.shape, 0)
    if grid_rows is None:
        taps = [(0, dc) for dc in (-1, 0, 1)]
        col = t_idx
        n_cols = t_len
        row = jnp.zeros_like(t_idx)
        n_rows = 1
    else:
        taps = [(dr, dc) for dr in (-1, 0, 1) for dc in (-1, 0, 1)]
        col = t_idx % GRID_W
        n_cols = GRID_W
        row = t_idx // GRID_W
        n_rows = grid_rows
    acc = jnp.zeros(x.shape, F32)
    for dr, dc in taps:
        off = dr * n_cols + dc
        xs = x if off == 0 else pltpu.roll(x, shift=(-off) % t_len, axis=0)
        ok = ((col + dc >= 0) & (col + dc < n_cols) & (row + dr >= 0) & (row + dr < n_rows))
        wv = w_ref[dr + 1, dc + 1:dc + 2, :]
        acc = acc + jnp.where(ok, xs, 0.0) * wv
    y = acc + b_ref[...]
    y = y * jax.nn.sigmoid(y)
    scale = jnp.where(pl.program_id(1) < n_q_blocks, q_scale, 1.0)
    o_ref[...] = (y * scale).astype(o_ref.dtype)


def _qk_conv(proj, conv_w, conv_b, row_off, n_seq, seq_len, grid_rows, dk):
    ch = conv_w.shape[-1]
    tc = _pick(ch // 2, (512, 256, 128))
    n_q_blocks = (ch // 2) // tc
    body = functools.partial(_conv_body, grid_rows, n_q_blocks, float(dk) ** -0.5)
    est = 2 * (_nbytes((seq_len, tc), F32) + _nbytes((seq_len, tc), BF16)) + 8 * _nbytes((seq_len, tc), F32)
    return pl.pallas_call(
        body,
        grid=(n_seq, ch // tc),
        in_specs=[pl.BlockSpec((seq_len, tc), lambda b, c: (row_off // seq_len + b, c)),
                  pl.BlockSpec((3, 3, tc), lambda b, c: (0, 0, c)),
                  pl.BlockSpec((1, tc), lambda b, c: (0, c))],
        out_specs=pl.BlockSpec((seq_len, tc), lambda b, c: (b, c)),
        out_shape=jax.ShapeDtypeStruct((n_seq * seq_len, ch), BF16),
        compiler_params=_cparams(2, est),
        name="qk_conv",
    )(proj, conv_w, conv_b.reshape(1, ch))


def _log_sigmoid(x):
    return jnp.minimum(x, 0.0) - jnp.log1p(jnp.exp(-jnp.abs(x)))


def _mlstm_decay(backward, li_col, fp_col, li_row, fp_row, m_prev):
    l = li_col.shape[0]
    lf_col = _log_sigmoid(fp_col)
    lf_row = _log_sigmoid(fp_row)
    t_i = lax.broadcasted_iota(jnp.int32, (l, l), 0)
    s_i = lax.broadcasted_iota(jnp.int32, (l, l), 1)
    causal = (s_i >= t_i) if backward else (s_i <= t_i)
    causal_t = (t_i >= s_i) if backward else (t_i <= s_i)
    b_col = jnp.sum(jnp.where(causal, lf_row, 0.0), axis=1, keepdims=True)
    b_row = jnp.sum(jnp.where(causal_t, lf_col, 0.0), axis=0, keepdims=True)
    b_last = jnp.sum(lf_col, axis=0, keepdims=True)
    log_d = jnp.where(causal, b_col - b_row + li_row, -jnp.inf)
    log_inter = b_col + m_prev
    m_out = jnp.maximum(log_inter, jnp.max(log_d, axis=1, keepdims=True))
    dmat = jnp.exp(log_d - m_out)
    s_inter = jnp.exp(log_inter - m_out)
    log_w = b_last - b_col + li_col
    m_new = jnp.maximum(b_last + m_prev, jnp.max(log_w, axis=0, keepdims=True))
    w = jnp.exp(log_w - m_new)
    decay = jnp.exp(b_last + m_prev - m_new)
    return dmat, s_inter, jnp.exp(-m_out), w, decay, m_new


def _mlstm_chunk(chains):
    qk = [lax.dot_general(ch["q"], ch["k"], (((1,), (1,)), ((), ())), preferred_element_type=F32)
          for ch in chains]
    qc = [None if ch["c"] is None else _dot(ch["q"], ch["c"].astype(BF16)) for ch in chains]
    dec = [_mlstm_decay(ch["backward"], ch["li_col"], ch["fp_col"], ch["li_row"], ch["fp_row"], ch["m"])
           for ch in chains]
    s, den, kw, n_new = [], [], [], []
    for ch, (dmat, s_inter, _, w, decay, _), qk_i in zip(chains, dec, qk):
        s_i = qk_i * dmat
        s.append(s_i)
        den_i = jnp.sum(s_i, axis=1, keepdims=True)
        kw_i = ch["k"].astype(F32) * w
        n_i = jnp.sum(kw_i, axis=0, keepdims=True)
        if ch["n"] is not None:
            den_i = den_i + s_inter * jnp.sum(ch["q"].astype(F32) * ch["n"], axis=1, keepdims=True)
            n_i = decay * ch["n"] + n_i
        den.append(den_i)
        kw.append(kw_i)
        n_new.append(n_i)
    sv = [_dot(s_i.astype(BF16), ch["v"]) for ch, s_i in zip(chains, s)]
    kv = [lax.dot_general(kw_i.astype(BF16), ch["v"], (((0,), (0,)), ((), ())), preferred_element_type=F32)
          for ch, kw_i in zip(chains, kw)]
    out = []
    for ch, (_, s_inter, exp_neg_m, _, decay, m_new), qc_i, den_i, sv_i, kv_i, n_i in zip(
            chains, dec, qc, den, sv, kv, n_new):
        num = sv_i if qc_i is None else sv_i + s_inter * qc_i
        h = num / jnp.maximum(jnp.abs(den_i), exp_neg_m)
        out.append((h, kv_i if ch["c"] is None else decay * ch["c"] + kv_i, n_i, m_new))
    return out


def _mlstm_body(zero_init, emit_state, single_chunk, n_heads, dk, dv, *refs):
    it = iter(refs)
    fwd_in = tuple(next(it) for _ in range(5))
    bwd_in = tuple(next(it) for _ in range(5))
    if not zero_init:
        c0, n0, m0 = next(it), next(it), next(it)
    if emit_state:
        next(it), next(it), next(it)
    hf_o, hb_o = next(it), next(it)
    if emit_state:
        c_o, n_o, m_o = next(it), next(it), next(it)
    carried = not (zero_init and single_chunk)
    if carried:
        c_s, n_s, m_s = next(it), next(it), next(it)
        c_idx = pl.program_id(2)

        @pl.when(c_idx == 0)
        def _():
            if zero_init:
                c_s[...] = jnp.zeros(c_s.shape, F32)
                n_s[...] = jnp.zeros(n_s.shape, F32)
                m_s[...] = jnp.zeros(m_s.shape, F32)
            else:
                c_s[...] = c0[...]
                n_s[...] = n0[...]
                m_s[...] = m0[...]

        n_all, m_all = n_s[...], m_s[...]
    else:
        m_all = jnp.zeros((2, n_heads), F32)
    head_row = lax.broadcasted_iota(jnp.int32, (n_heads, dk), 0)
    m_dir = lax.broadcasted_iota(jnp.int32, m_all.shape, 0)
    m_head = lax.broadcasted_iota(jnp.int32, m_all.shape, 1)
    chains = []
    for d, (q_ref, k_ref, v_ref, gc_ref, gr_ref) in enumerate((fwd_in, bwd_in)):
        for h in range(n_heads):
            gi, gf = (2 * d) * n_heads + h, (2 * d + 1) * n_heads + h
            chains.append(dict(
                backward=(d == 1),
                q=q_ref[:, h * dk:(h + 1) * dk], k=k_ref[:, h * dk:(h + 1) * dk],
                v=v_ref[:, h * dv:(h + 1) * dv],
                li_col=gc_ref[:, gi:gi + 1], fp_col=gc_ref[:, gf:gf + 1],
                li_row=gr_ref[gi:gi + 1, :], fp_row=gr_ref[gf:gf + 1, :],
                c=c_s[d, h] if carried else None, n=n_all[d, h:h + 1, :] if carried else None,
                m=m_all[d:d + 1, h:h + 1]))
    results = _mlstm_chunk(chains)
    m_next = m_all
    n_next = []
    for d, h_o in enumerate((hf_o, hb_o)):
        n_d = jnp.zeros((n_heads, dk), F32)
        for h in range(n_heads):
            hh, c_new, n_new, m_new = results[d * n_heads + h]
            h_o[:, h * dv:(h + 1) * dv] = hh
            if carried:
                c_s[d, h] = c_new
            else:
                c_o[d, h] = c_new
            n_d = jnp.where(head_row == h, n_new, n_d)
            m_next = jnp.where((m_dir == d) & (m_head == h), m_new, m_next)
        n_next.append(n_d)
    if carried:
        for d in range(2):
            n_s[d] = n_next[d]
        m_s[...] = m_next
        if emit_state:
            @pl.when(c_idx == pl.num_programs(2) - 1)
            def _():
                c_o[...] = c_s[...]
                n_o[...] = n_s[...]
                m_o[...] = m_s[...]
    else:
        for d in range(2):
            n_o[d] = n_next[d]
        m_o[...] = m_next


def _mlstm_scan(qk, v, gates, row_off, n_seq, seq_len, dk, dv, layer_j, state_in=None, state_out=None):
    nh = M_HEADS
    hb = _pick(nh, (M_HEADS_PER_STEP, 2, 1))
    n_hg = nh // hb
    lc = _pick(seq_len, (M_SCAN_CHUNK, 128, 64, 32, 16, 8))
    nc = seq_len // lc
    n_rows = n_seq * seq_len
    zero_init = state_in is None
    emit_state = state_out is not None
    carried = not (zero_init and nc == 1)
    assert carried or emit_state
    n_tot = gates.shape[0]
    g5 = gates.reshape(n_tot // lc, lc, M_GATES, n_hg, hb)
    g_col = jnp.transpose(g5, (3, 0, 1, 2, 4)).reshape(n_hg, n_tot // lc, lc, M_GATES * hb)
    g_row = jnp.swapaxes(g_col, 2, 3)
    off_c = row_off // lc

    def specs(cmap):
        return [pl.BlockSpec((lc, hb * dk), lambda b, g, c: (cmap(b, c), g)),
                pl.BlockSpec((lc, hb * dk), lambda b, g, c: (cmap(b, c), n_hg + g)),
                pl.BlockSpec((lc, hb * dv), lambda b, g, c: (off_c + cmap(b, c), g)),
                pl.BlockSpec((None, None, lc, M_GATES * hb), lambda b, g, c: (g, off_c + cmap(b, c), 0, 0)),
                pl.BlockSpec((None, None, M_GATES * hb, lc), lambda b, g, c: (g, off_c + cmap(b, c), 0, 0))]

    def fwd(b, c):
        return b * nc + c

    def bwd(b, c):
        return b * nc + (nc - 1 - c)

    def group_views(state):
        c_a, n_a, m_a = state
        lead = c_a.shape[:3]
        return (c_a.reshape(lead + (n_hg, hb, dk, dv)), n_a.reshape(lead + (n_hg, hb, dk)),
                jnp.swapaxes(m_a.reshape(lead + (n_hg, hb)), 2, 3))

    def state_specs():
        return [pl.BlockSpec((None, None, 2, None, hb, dk, dv), lambda b, g, c: (b, layer_j, 0, g, 0, 0, 0)),
                pl.BlockSpec((None, None, 2, None, hb, dk), lambda b, g, c: (b, layer_j, 0, g, 0, 0)),
                pl.BlockSpec((None, None, None, 2, hb), lambda b, g, c: (b, layer_j, g, 0, 0))]

    in_specs = specs(fwd) + specs(bwd)
    args = [qk, qk, v, g_col, g_row] * 2
    if not zero_init:
        in_specs += state_specs()
        args += list(group_views(state_in))
    aliases = {}
    if emit_state:
        out_views = group_views(state_out)
        for t, s in enumerate(out_views):
            aliases[len(args)] = 2 + t
            in_specs.append(pl.BlockSpec(memory_space=pl.ANY))
            args.append(s)
    out_specs = [pl.BlockSpec((lc, hb * dv), lambda b, g, c: (fwd(b, c), g)),
                 pl.BlockSpec((lc, hb * dv), lambda b, g, c: (bwd(b, c), g))]
    out_shape = [jax.ShapeDtypeStruct((n_rows, nh * dv), F32)] * 2
    if emit_state:
        out_specs += state_specs()
        out_shape += [jax.ShapeDtypeStruct(s.shape, F32) for s in out_views]
    scratch = []
    if carried:
        scratch = [pltpu.VMEM((2, hb, dk, dv), F32), pltpu.VMEM((2, hb, dk), F32), pltpu.VMEM((2, hb), F32)]
    state_bytes = _nbytes((2, hb, dk, dv), F32)
    chain_bytes = 4 * _nbytes((lc, lc), F32) + 4 * _nbytes((lc, dv), F32) + 2 * _nbytes((dk, dv), F32)
    est = (state_bytes * ((1 if carried else 0) + (2 if not zero_init else 0) + (2 if emit_state else 0))
           + 2 * hb * chain_bytes + 8 * _nbytes((lc, hb * dv), F32) + 16 * _nbytes((lc, hb * dk), BF16)
           + (4 << 20))
    outs = pl.pallas_call(
        functools.partial(_mlstm_body, zero_init, emit_state, nc == 1, hb, dk, dv),
        grid=(n_seq, n_hg, nc),
        in_specs=in_specs,
        out_specs=out_specs,
        out_shape=out_shape,
        scratch_shapes=scratch,
        input_output_aliases=aliases,
        compiler_params=_cparams(3, est),
        name="mlstm_scan",
    )(*args)
    new_state = None
    if emit_state:
        c_a, n_a, m_a = state_out
        new_state = (outs[2].reshape(c_a.shape), outs[3].reshape(n_a.shape),
                     jnp.swapaxes(outs[4], 2, 3).reshape(m_a.shape))
    return outs[0], outs[1], new_state


def _mlstm_combine_body(hf_ref, hb_ref, o_ref, g_ref, z_ref):
    hh = _norm_rows(hf_ref[...] + hb_ref[...], g_ref[...])
    z_ref[...] = (hh * jax.nn.sigmoid(o_ref[...])).astype(z_ref.dtype)


def _mlstm_combine(h_f, h_b, o_gate, head_g, dv, row_off):
    n_rows, width = h_f.shape
    tm = _pick(math.gcd(n_rows, row_off) if row_off else n_rows, (512, 256, 128, 64, 32, 16, 8))
    est = 2 * (3 * _nbytes((tm, dv), F32) + _nbytes((tm, dv), BF16)) + 4 * _nbytes((tm, dv), F32)
    return pl.pallas_call(
        _mlstm_combine_body,
        grid=(n_rows // tm, M_HEADS),
        in_specs=[pl.BlockSpec((tm, dv), lambda i, h: (i, h)),
                  pl.BlockSpec((tm, dv), lambda i, h: (i, h)),
                  pl.BlockSpec((tm, dv), lambda i, h: (row_off // tm + i, h)),
                  pl.BlockSpec((1, dv), lambda i, h: (0, h))],
        out_specs=pl.BlockSpec((tm, dv), lambda i, h: (i, h)),
        out_shape=jax.ShapeDtypeStruct((n_rows, width), BF16),
        compiler_params=_cparams(2, est),
        name="mlstm_combine",
    )(h_f, h_b, o_gate, head_g.reshape(1, width))


def _s5_body(zero_init, emit_y, emit_state, seq_len, tc_len, *refs):
    it = iter(refs)
    u_ref, bw_ref, ar_ref, ai_ref = (next(it) for _ in range(4))
    cw_ref = next(it) if emit_y else None
    if not zero_init:
        x0r_ref, x0i_ref = next(it), next(it)
    y_ref = next(it) if emit_y else None
    if emit_state:
        fr_ref, fi_ref = next(it), next(it)
    e_s = next(it)
    xb_s = next(it) if emit_y else None
    ns = ar_ref.shape[-1]
    cin = u_ref.shape[-1]
    n_tc = seq_len // tc_len
    rows_c = tc_len * S5_ROWS
    pair_rows = 2 * S5_ROWS

    for d in range(2):
        ar = jnp.broadcast_to(ar_ref[d], (S5_ROWS, ns))
        ai = jnp.broadcast_to(ai_ref[d], (S5_ROWS, ns))

        def step(r, xr, xi):
            er = e_s[pl.ds(r, S5_ROWS), 0:ns]
            ei = e_s[pl.ds(r, S5_ROWS), ns:2 * ns]
            return ar * xr - ai * xi + er, ar * xi + ai * xr + ei

        if zero_init:
            xr = xi = jnp.zeros((S5_ROWS, ns), F32)
        else:
            xr, xi = x0r_ref[d], x0i_ref[d]
        for c in (range(n_tc) if d == 0 else range(n_tc - 1, -1, -1)):
            u2 = u_ref[c * tc_len:(c + 1) * tc_len].reshape(rows_c, cin).astype(BF16)
            e_s[...] = _dot(u2, bw_ref[d])

            def pair(p, carry, d=d, c=c):
                q = p if d == 0 else tc_len // 2 - 1 - p
                r_lo = pl.multiple_of(q * pair_rows, pair_rows)
                r_hi = r_lo + S5_ROWS
                x1 = step(r_lo if d == 0 else r_hi, *carry)
                x2 = step(r_hi if d == 0 else r_lo, *x1)
                if emit_y:
                    lo, hi = (x1, x2) if d == 0 else (x2, x1)
                    r_out = pl.multiple_of(c * rows_c + r_lo, pair_rows)
                    for part in range(2):
                        col = (2 * d + part) * ns
                        xb_s[pl.ds(r_out, pair_rows), col:col + ns] = (
                            jnp.concatenate([lo[part], hi[part]], axis=0).astype(BF16))
                return x2

            xr, xi = lax.fori_loop(0, tc_len // 2, pair, (xr, xi), unroll=4)
        if emit_state:
            fr_ref[d] = xr
            fi_ref[d] = xi

    if emit_y:
        y = _dot(xb_s[...], cw_ref[...])
        y_ref[...] = y.reshape(y_ref.shape)


def _s5_scan(u_tm, bw, a_re, a_im, cw, state, emit_y, emit_state):
    seq_len, n_pad, e_width = u_tm.shape
    n_sb = bw.shape[1]
    cin = e_width // n_sb
    ns = bw.shape[-1] // 2
    tc_len = _pick(seq_len, (128, 64, 32, 16, 8, 4, 2))
    n_bg = n_pad // S5_ROWS
    zero_init = state is None

    def wspec(r, c):
        return pl.BlockSpec((2, None, r, c), lambda bg, sb: (0, sb, 0, 0))

    def sspec():
        return pl.BlockSpec((2, S5_ROWS, ns), lambda bg, sb: (0, bg, sb))

    in_specs = [pl.BlockSpec((seq_len, S5_ROWS, cin), lambda bg, sb: (0, bg, sb)),
                wspec(cin, 2 * ns), wspec(1, ns), wspec(1, ns)]
    args = [u_tm, bw, a_re, a_im]
    if emit_y:
        in_specs.append(pl.BlockSpec((None, 4 * ns, cin), lambda bg, sb: (sb, 0, 0)))
        args.append(cw)
    if not zero_init:
        in_specs += [sspec(), sspec()]
        args += list(state)
    out_specs, out_shape = [], []
    scratch = [pltpu.VMEM((tc_len * S5_ROWS, 2 * ns), F32)]
    if emit_y:
        out_specs.append(pl.BlockSpec((seq_len, S5_ROWS, cin), lambda bg, sb: (0, bg, sb)))
        out_shape.append(jax.ShapeDtypeStruct((seq_len, n_pad, e_width), F32))
        scratch.append(pltpu.VMEM((seq_len * S5_ROWS, 4 * ns), BF16))
    if emit_state:
        out_specs += [sspec(), sspec()]
        out_shape += [jax.ShapeDtypeStruct((2, n_pad, n_sb * ns), F32)] * 2
    est = (2 * _nbytes((tc_len * S5_ROWS, 2 * ns), F32) + _nbytes((seq_len * S5_ROWS, 4 * ns), BF16)
           + 5 * _nbytes((seq_len, S5_ROWS, cin), F32) + 8 * _nbytes((cin, 2 * ns), BF16) + (4 << 20))
    outs = pl.pallas_call(
        functools.partial(_s5_body, zero_init, emit_y, emit_state, seq_len, tc_len),
        grid=(n_bg, n_sb),
        in_specs=in_specs,
        out_specs=out_specs,
        out_shape=out_shape,
        scratch_shapes=scratch,
        compiler_params=_cparams(2, est),
        name="s5_scan",
    )(*args)
    y = outs[0] if emit_y else None
    fin = tuple(outs[-2:]) if emit_state else None
    return y, fin


def _s5_carry_body(n_seq, n_seg, fr_ref, fi_ref, x0r_ref, x0i_ref, pr_ref, pi_ref, or_ref, oi_ref):
    or_ref[...] = jnp.zeros(or_ref.shape, F32)
    oi_ref[...] = jnp.zeros(oi_ref.shape, F32)
    for d in range(2):
        pr, pi = pr_ref[d], pi_ref[d]
        order = range(n_seg) if d == 0 else range(n_seg - 1, -1, -1)
        for b in range(n_seq):
            cr, ci = x0r_ref[d, b:b + 1, :], x0i_ref[d, b:b + 1, :]
            for k in order:
                r = b * n_seg + k
                or_ref[d, r:r + 1, :] = cr
                oi_ref[d, r:r + 1, :] = ci
                fr, fi = fr_ref[d, r:r + 1, :], fi_ref[d, r:r + 1, :]
                cr, ci = pr * cr - pi * ci + fr, pr * ci + pi * cr + fi


def _s5_carry(fin, x0, a_pow, n_seq, n_seg):
    shape = fin[0].shape
    return pl.pallas_call(
        functools.partial(_s5_carry_body, n_seq, n_seg),
        out_shape=[jax.ShapeDtypeStruct(shape, F32)] * 2,
        compiler_params=_cparams(0, 16 * _nbytes(shape, F32)),
        name="s5_carry",
    )(fin[0], fin[1], x0[0], x0[1], a_pow[0], a_pow[1])


def _s5_gelu_body(y_ref, u_ref, d_ref, z_ref, v_s):
    c = math.sqrt(2.0 / math.pi)
    v_s[...] = y_ref[...] + d_ref[...] * u_ref[...]
    for b in range(S5_ROWS):
        v = v_s[:, b, :]
        z = 0.5 * v * (1.0 + jnp.tanh(c * (v + 0.044715 * (v * v * v))))
        z_ref[b] = z.astype(z_ref.dtype)


def _s5_gelu(y, u_tm, d_skip):
    seq_len, n_pad, e_width = u_tm.shape
    tt = _pick(seq_len, (128, 64, 32, 16, 8))
    ec = _pick(e_width, (512, 256, 128))
    blk = (tt, S5_ROWS, ec)
    est = 2 * (2 * _nbytes(blk, F32) + _nbytes(blk, BF16)) + 8 * _nbytes(blk, F32)
    return pl.pallas_call(
        _s5_gelu_body,
        grid=(n_pad // S5_ROWS, seq_len // tt, e_width // ec),
        in_specs=[pl.BlockSpec(blk, lambda bg, t, e: (t, bg, e)),
                  pl.BlockSpec(blk, lambda bg, t, e: (t, bg, e)),
                  pl.BlockSpec((1, ec), lambda bg, t, e: (0, e))],
        out_specs=pl.BlockSpec((S5_ROWS, tt, ec), lambda bg, t, e: (bg, t, e)),
        out_shape=jax.ShapeDtypeStruct((n_pad, seq_len, e_width), BF16),
        scratch_shapes=[pltpu.VMEM(blk, F32)],
        compiler_params=_cparams(3, est),
        name="s5_gelu",
    )(y, u_tm, d_skip.reshape(1, e_width))


def _s5_weights(lam_re, lam_im, log_dt, b_re, b_im, c_re, c_im, seg_len):
    lr = jnp.minimum(lam_re.astype(F32), -1e-4)
    li = lam_im.astype(F32)
    dt = jnp.exp(log_dt.astype(F32))[..., None]
    mag = jnp.exp(lr * dt)
    ar = mag * jnp.cos(li * dt)
    ai = mag * jnp.sin(li * dt)
    den = lr * lr + li * li
    xr = ar - 1.0
    cr = (xr * lr + ai * li) / den
    ci = (ai * lr - xr * li) / den
    bp_re = cr[..., None] * b_re - ci[..., None] * b_im
    bp_im = cr[..., None] * b_im + ci[..., None] * b_re
    n_dir, g, p = lam_re.shape
    gc = b_re.shape[-1]
    sbg = min(S5_SB_GROUPS, g)
    n_sb = g // sbg
    eye = jnp.eye(sbg, dtype=F32)

    def bd_in(bp):
        t = bp.reshape(n_dir, n_sb, sbg, p, gc)
        t = jnp.einsum('dsgpc,gh->dsgchp', t, eye)
        return t.reshape(n_dir, n_sb, sbg * gc, sbg * p)

    def bd_out(cc):
        t = cc.reshape(n_dir, n_sb, sbg, gc, p)
        t = jnp.einsum('dsgcp,gh->dsgphc', t, eye)
        return t.reshape(n_dir, n_sb, sbg * p, sbg * gc)

    bw = jnp.concatenate([bd_in(bp_re), bd_in(bp_im)], axis=-1).astype(BF16)
    cw = jnp.concatenate([bd_out(c_re.astype(F32)), -bd_out(c_im.astype(F32))], axis=-2).astype(BF16)
    cw = jnp.swapaxes(cw, 0, 1).reshape(n_sb, n_dir * 2 * sbg * p, sbg * gc)
    a_re = ar.reshape(n_dir, n_sb, 1, sbg * p)
    a_im = ai.reshape(n_dir, n_sb, 1, sbg * p)
    pr, pi = jnp.ones_like(ar), jnp.zeros_like(ai)
    br, bi = ar, ai
    e = seg_len
    while e:
        if e & 1:
            pr, pi = pr * br - pi * bi, pr * bi + pi * br
        br, bi = br * br - bi * bi, 2.0 * br * bi
        e >>= 1
    a_pow = (pr.reshape(n_dir, 1, g * p), pi.reshape(n_dir, 1, g * p))
    return bw, a_re, a_im, cw, a_pow


def _s5_group(x, g, mod, l, w_in, lead, weights, d_skip, row_off, n_seq, seq_len, cond0, per_seq_cond,
              state, want_state):
    bw, a_re, a_im, cw, a_pow = weights
    d = x.shape[1]
    seg = _pick(seq_len, (S5_SEG, 128, 64, 32, 16, 8))
    n_seg = seq_len // seg
    n_ps = n_seq * n_seg
    n_pad = -(-n_ps // S5_ROWS) * S5_ROWS

    def cond_of(p):
        return cond0 + (p // n_seg if per_seq_cond else 0)

    h_tm = _norm_mod_tm(x, g, mod, l, 1, row_off, n_ps, seg, n_pad, cond_of)
    e_w = w_in.shape[-1]
    u_tm = _mm("plain", h_tm.reshape(seg * n_pad, d), w_in, lead, [lambda j: j],
               _pick(e_w, (512, 256, 128)), e_w, F32).reshape(seg, n_pad, e_w)

    def pad_rows(s):
        return jnp.pad(s, ((0, 0), (0, n_pad - s.shape[1]), (0, 0)))

    if n_seg == 1:
        start = None if state is None else tuple(pad_rows(s) for s in state)
    else:
        assert not want_state
        _, fin = _s5_scan(u_tm, bw, a_re, a_im, cw, None, emit_y=False, emit_state=True)
        x0 = state if state is not None else tuple(jnp.zeros((2, n_seq, fin[0].shape[-1]), F32) for _ in range(2))
        start = _s5_carry(fin, x0, a_pow, n_seq, n_seg)
    y, fin = _s5_scan(u_tm, bw, a_re, a_im, cw, start, emit_y=True, emit_state=want_state)
    z = _s5_gelu(y, u_tm, d_skip).reshape(n_pad * seg, e_w)[:n_seq * seq_len]
    if want_state:
        fin = tuple(f[:, :n_seq] for f in fin)
    return z, fin


def kernel(x_prompt, x_sample, state_mlstm_C, state_mlstm_n, state_mlstm_m, state_s5_re, state_s5_im, c, c_ctx, w_ada, b_ada, norm_g, final_g, w_ffn_in, w_ffn_out, m_w_in, m_conv_w, m_conv_b, m_gate_b, m_head_g, m_w_out, s5_w_in, s5_lam_re, s5_lam_im, s5_log_dt, s5_b_re, s5_b_im, s5_c_re, s5_c_im, s5_d, s5_w_out):
    bp, tp, d = x_prompt.shape
    bs, ts, _ = x_sample.shape
    depth = w_ada.shape[0]
    n_ctx, n_lat = bp * tp, bs * ts
    rows = _Rows(n_ctx, bs, ts)
    tm = rows.tile()
    x = jnp.concatenate([x_prompt.reshape(n_ctx, d), x_sample.reshape(n_lat, d)], axis=0)

    cond = jnp.concatenate([c_ctx[None, :], c], axis=0).astype(F32)
    mod = _ada(cond, w_ada, b_ada).reshape(depth, 1 + bs, N_SUB * N_MOD, 1, d)

    n_ml, n_s5 = m_w_in.shape[0], s5_w_in.shape[0]
    dk, dv = state_mlstm_C.shape[-2], state_mlstm_C.shape[-1]
    qk_w = 2 * M_HEADS * dk
    v_w = M_HEADS * dv
    main_w = qk_w + 2 * v_w
    g_groups, p_state = state_s5_re.shape[-2], state_s5_re.shape[-1]
    d_ff = w_ffn_out.shape[-2]

    def gate_of(l, sub):
        return (mod, rows, l, sub * N_MOD + 2)

    def ffn(x, l, idx, sub):
        h = _norm_mod(x, norm_g[l, sub], mod, rows, l, sub)
        tn = _pick(d_ff, (256, 128))
        nf = d_ff // tn
        a = _mm("swiglu", h, w_ffn_in, (l, idx), [lambda j: j, lambda j: nf + j], tn, d_ff, BF16)
        k_parts = 2 if d_ff % (2 * LANES) == 0 and d_ff > 4096 else 1
        for p in range(k_parts):
            x = _mm("resid", a, w_ffn_out, (l, idx), [lambda j: j], _pick(d, (256, 128)), d, F32,
                    k_part=(p, k_parts), coef=0.5, res=x, gate=gate_of(l, sub), tm=tm)
        return x

    def mlstm(x, l, j, new_state):
        h = _norm_mod(x, norm_g[l, 1], mod, rows, l, 1)
        tn = _pick(math.gcd(qk_w, v_w), (512, 256, 128))
        qk_pre = _mm("plain", h, m_w_in, (j,), [lambda jb: jb], tn, qk_w, F32)
        v = _mm("plain", h, m_w_in, (j,), [lambda jb: qk_w // tn + jb], tn, v_w, BF16)
        o_gate = _mm("plain", h, m_w_in, (j,), [lambda jb: (qk_w + v_w) // tn + jb], tn, v_w, F32)
        w_g = jnp.pad(m_w_in[j, :, main_w:], ((0, 0), (0, LANES - M_GATES * M_HEADS)))
        gates = _mm("plain", h, w_g, (), [lambda jb: jb], LANES, LANES, F32)[:, :M_GATES * M_HEADS]
        gates = gates + m_gate_b[j].reshape(1, M_GATES * M_HEADS).astype(F32)
        qk_p = _qk_conv(qk_pre, m_conv_w[j], m_conv_b[j], 0, bp, tp, None, dk)
        qk_s = _qk_conv(qk_pre, m_conv_w[j], m_conv_b[j], n_ctx, bs, ts, ts // GRID_W, dk)
        hf_p, hb_p, new_state = _mlstm_scan(qk_p, v, gates, 0, bp, tp, dk, dv, j, None, new_state)
        cached = (state_mlstm_C, state_mlstm_n, state_mlstm_m)
        hf_s, hb_s, _ = _mlstm_scan(qk_s, v, gates, n_ctx, bs, ts, dk, dv, j, cached, None)
        z = jnp.concatenate([_mlstm_combine(hf_p, hb_p, o_gate, m_head_g[j], dv, 0),
                             _mlstm_combine(hf_s, hb_s, o_gate, m_head_g[j], dv, n_ctx)], axis=0)
        x = _mm("resid", z, m_w_out, (j,), [lambda jb: jb], _pick(d, (512, 256, 128)), d, F32,
                res=x, gate=gate_of(l, 1), tm=tm)
        return x, new_state

    def s5(x, l, j):
        seg = _pick(ts, (S5_SEG, 128, 64, 32, 16, 8))
        weights = _s5_weights(s5_lam_re[j], s5_lam_im[j], s5_log_dt[j], s5_b_re[j], s5_b_im[j],
                              s5_c_re[j], s5_c_im[j], seg)
        cached = tuple(jnp.swapaxes(s[:, j], 0, 1).reshape(2, bs, g_groups * p_state).astype(F32)
                       for s in (state_s5_re, state_s5_im))
        z_p, fin = _s5_group(x, norm_g[l, 1], mod, l, s5_w_in, (j,), weights, s5_d[j],
                             0, bp, tp, 0, False, None, True)
        z_s, _ = _s5_group(x, norm_g[l, 1], mod, l, s5_w_in, (j,), weights, s5_d[j],
                           n_ctx, bs, ts, 1, True, cached, False)
        z = jnp.concatenate([z_p, z_s], axis=0)
        tn = _pick(d, (256, 128))
        nd = d // tn
        x = _mm("glu_resid", z, s5_w_out, (j,), [lambda jb: jb, lambda jb: nd + jb], tn, d, F32,
                res=x, gate=gate_of(l, 1), tm=tm)
        fin = tuple(jnp.swapaxes(f, 0, 1).reshape(bp, 2, g_groups, p_state) for f in fin)
        return x, fin

    new_ml = (jnp.zeros((bp, n_ml, 2, M_HEADS, dk, dv), F32), jnp.zeros((bp, n_ml, 2, M_HEADS, dk), F32),
              jnp.zeros((bp, n_ml, 2, M_HEADS), F32))
    new_re, new_im = [], []
    for l in range(depth):
        x = ffn(x, l, 0, 0)
        if l % 2 == 0:
            x, new_ml = mlstm(x, l, l // 2, new_ml)
        else:
            x, (f_re, f_im) = s5(x, l, l // 2)
            new_re.append(f_re)
            new_im.append(f_im)
        x = ffn(x, l, 1, 2)

    y_prompt = _final_norm(x, final_g, 0, n_ctx).reshape(bp, tp, d)
    y_sample = _final_norm(x, final_g, n_ctx, n_lat).reshape(bs, ts, d)
    return (y_prompt, y_sample) + tuple(new_ml) + (jnp.stack(new_re, axis=1), jnp.stack(new_im, axis=1))
```

```python
import functools
import math

import jax
import jax.numpy as jnp
from jax import lax
from jax.experimental import pallas as pl
from jax.experimental.pallas import tpu as pltpu

F32 = jnp.float32
BF16 = jnp.bfloat16

NORM_EPS = 1e-6
N_SUB = 3
N_MOD = 3
GRID_W = 64
M_HEADS = 8
M_GATES = 4
M_SCAN_CHUNK = 256
M_HEADS_PER_STEP = 4
S5_SB_GROUPS = 16
S5_ROWS = 8
S5_SEG = 256

V7X_VMEM_LIMIT_BYTES = 60000 * 1024
LANES = 128
SUBLANES = 8
ROW_TILES = (1024, 512, 256, 128, 64, 32, 16, 8)


def _pick(n, prefs):
    for p in prefs:
        if p <= n and n % p == 0:
            return p
    return n


def _cparams(n_axes, est_bytes):
    limit = int(min(V7X_VMEM_LIMIT_BYTES, max(est_bytes, 16 * 1024 * 1024)))
    return pltpu.CompilerParams(dimension_semantics=("arbitrary",) * n_axes,
                                vmem_limit_bytes=limit)


def _nbytes(shape, dtype):
    return math.prod(shape) * jnp.dtype(dtype).itemsize


def _ada_body(n_cond, sb_ref, w_ref, b_ref, o_ref):
    k_dim, tn = w_ref.shape
    nl = tn // LANES

    def step(kc, acc):
        k0 = pl.multiple_of(kc * SUBLANES, SUBLANES)
        s = [sb_ref[c, pl.ds(k0, SUBLANES), :] for c in range(n_cond)]
        ws = [w_ref[pl.ds(k0, SUBLANES), l * LANES:(l + 1) * LANES] for l in range(nl)]
        return tuple(acc[c * nl + l] + ws[l] * s[c] for c in range(n_cond) for l in range(nl))

    acc0 = tuple(jnp.zeros((SUBLANES, LANES), F32) for _ in range(n_cond * nl))
    acc = lax.fori_loop(0, k_dim // SUBLANES, step, acc0, unroll=4)
    o_ref[...] = jnp.zeros(o_ref.shape, F32)
    for c in range(n_cond):
        for l in range(nl):
            o_ref[c:c + 1, l * LANES:(l + 1) * LANES] = (
                jnp.sum(acc[c * nl + l], axis=0, keepdims=True) + b_ref[:, l * LANES:(l + 1) * LANES])


def _ada(cond, w_ada, b_ada):
    n_cond, d = cond.shape
    depth, _, n_out = w_ada.shape
    sc = cond * jax.nn.sigmoid(cond)
    sb = jnp.broadcast_to(sc[:, :, None], (n_cond, d, LANES))
    tn = _pick(n_out, (1024, 512, 256, 128))
    est = 2 * (_nbytes((d, tn), F32) + _nbytes((n_cond, d, LANES), F32)) + (4 << 20)
    out = pl.pallas_call(
        functools.partial(_ada_body, n_cond),
        grid=(depth, n_out // tn),
        in_specs=[pl.BlockSpec((n_cond, d, LANES), lambda l, j: (0, 0, 0)),
                  pl.BlockSpec((None, d, tn), lambda l, j: (l, 0, j)),
                  pl.BlockSpec((None, 1, tn), lambda l, j: (l, 0, j))],
        out_specs=pl.BlockSpec((None, SUBLANES, tn), lambda l, j: (l, 0, j)),
        out_shape=jax.ShapeDtypeStruct((depth, SUBLANES, n_out), F32),
        compiler_params=_cparams(2, est),
        name="ada_mod",
    )(sb, w_ada, b_ada.reshape(depth, 1, n_out))
    return out[:, :n_cond]


def _norm_rows(x, g):
    ms = jnp.mean(x * x, axis=-1, keepdims=True)
    return x * lax.rsqrt(ms + NORM_EPS) * g


def _norm_mod_body(x_ref, g_ref, sh_ref, sc_ref, o_ref):
    y = _norm_rows(x_ref[...], g_ref[...])
    o_ref[...] = (y * (1.0 + sc_ref[...]) + sh_ref[...]).astype(o_ref.dtype)


def _norm_body(x_ref, g_ref, o_ref):
    o_ref[...] = _norm_rows(x_ref[...], g_ref[...]).astype(o_ref.dtype)


class _Rows:
    def __init__(self, n_ctx, n_lat_seq, lat_len):
        self.n_ctx = n_ctx
        self.lat_len = lat_len
        self.n = n_ctx + n_lat_seq * lat_len

    def tile(self, prefs=ROW_TILES):
        return _pick(math.gcd(self.n_ctx, self.lat_len), prefs)

    def cond_of_tile(self, i, tm):
        assert self.n_ctx % tm == 0 and self.lat_len % tm == 0
        pt = self.n_ctx // tm
        st = self.lat_len // tm
        return jnp.where(i < pt, 0, 1 + (i - pt) // st)


def _mod_spec(rows, tm, l, slot, tn):
    def imap(i, *rest):
        return (l, rows.cond_of_tile(i, tm), slot, 0, rest[0] if rest else 0)
    return pl.BlockSpec((None, None, None, 1, tn), imap)


def _norm_mod(x, g, mod, rows, l, sub):
    n, d = x.shape
    tm = rows.tile((256, 128, 64, 32, 16, 8))
    est = 2 * (_nbytes((tm, d), F32) + _nbytes((tm, d), BF16)) + 4 * _nbytes((tm, d), F32)
    return pl.pallas_call(
        _norm_mod_body,
        grid=(n // tm,),
        in_specs=[pl.BlockSpec((tm, d), lambda i: (i, 0)),
                  pl.BlockSpec((1, d), lambda i: (0, 0)),
                  _mod_spec(rows, tm, l, sub * N_MOD + 0, d),
                  _mod_spec(rows, tm, l, sub * N_MOD + 1, d)],
        out_specs=pl.BlockSpec((tm, d), lambda i: (i, 0)),
        out_shape=jax.ShapeDtypeStruct((n, d), BF16),
        compiler_params=_cparams(1, est),
        name="norm_mod",
    )(x, g.reshape(1, d), mod, mod)


def _norm_mod_tm(x, g, mod, l, sub, row_off, n_seq, seq_len, n_pad, cond_of_seq):
    n, d = x.shape
    tt = _pick(seq_len, (256, 128, 64, 32, 16, 8))
    nt = seq_len // tt

    def body(x_ref, g_ref, sh_ref, sc_ref, o_ref):
        b = pl.program_id(0)

        @pl.when(b < n_seq)
        def _():
            _norm_mod_body(x_ref, g_ref, sh_ref, sc_ref, o_ref)

        @pl.when(b >= n_seq)
        def _():
            o_ref[...] = jnp.zeros(o_ref.shape, o_ref.dtype)

    def x_map(b, t):
        return (row_off // tt + jnp.minimum(b, n_seq - 1) * nt + t, 0)

    def mod_spec(slot):
        return pl.BlockSpec((None, None, None, 1, d),
                            lambda b, t: (l, cond_of_seq(jnp.minimum(b, n_seq - 1)), slot, 0, 0))

    est = 2 * (_nbytes((tt, d), F32) + _nbytes((tt, d), BF16)) + 4 * _nbytes((tt, d), F32)
    return pl.pallas_call(
        body,
        grid=(n_pad, nt),
        in_specs=[pl.BlockSpec((tt, d), x_map),
                  pl.BlockSpec((1, d), lambda b, t: (0, 0)),
                  mod_spec(sub * N_MOD + 0),
                  mod_spec(sub * N_MOD + 1)],
        out_specs=pl.BlockSpec((tt, d), lambda b, t: (t, b)),
        out_shape=jax.ShapeDtypeStruct((seq_len, n_pad * d), BF16),
        compiler_params=_cparams(2, est),
        name="norm_mod_tm",
    )(x, g.reshape(1, d), mod, mod)


def _final_norm(x, g, row_off, n_rows):
    n, d = x.shape
    tm = _pick(math.gcd(row_off, n_rows) if row_off else n_rows, (256, 128, 64, 32, 16, 8))
    est = 8 * _nbytes((tm, d), F32)
    return pl.pallas_call(
        _norm_body,
        grid=(n_rows // tm,),
        in_specs=[pl.BlockSpec((tm, d), lambda i: (row_off // tm + i, 0)),
                  pl.BlockSpec((1, d), lambda i: (0, 0))],
        out_specs=pl.BlockSpec((tm, d), lambda i: (i, 0)),
        out_shape=jax.ShapeDtypeStruct((n_rows, d), F32),
        compiler_params=_cparams(1, est),
        name="final_norm",
    )(x, g.reshape(1, d))


def _dot(a, b):
    return jnp.dot(a, b, preferred_element_type=F32)


def _mm_body(kind, coef, n_w, has_side, w_t, *refs):
    a_ref, w_refs, rest = refs[0], refs[1:1 + n_w], refs[1 + n_w:]
    if has_side:
        side_in, rest, side_out = rest[0], rest[1:-1], rest[-1]
        side_out[...] = side_in[...].astype(BF16)
    ws = [w[...].astype(BF16) for w in w_refs]
    if w_t:
        acc = lax.dot_general(a_ref[...], ws[0], (((1,), (1,)), ((), ())), preferred_element_type=F32)
        tn = w_refs[0].shape[0]
    else:
        acc = _dot(a_ref[...], ws[0] if n_w == 1 else jnp.concatenate(ws, axis=1))
        tn = w_refs[0].shape[1]
    if kind == "plain":
        (o_ref,) = rest
        o_ref[...] = acc.astype(o_ref.dtype)
    elif kind == "swiglu":
        (o_ref,) = rest
        g, u = acc[:, :tn], acc[:, tn:]
        o_ref[...] = (g * jax.nn.sigmoid(g) * u).astype(o_ref.dtype)
    elif kind == "resid":
        res_ref, gate_ref, o_ref = rest
        o_ref[...] = res_ref[...] + (coef * gate_ref[...]) * acc
    elif kind == "glu_resid":
        res_ref, gate_ref, o_ref = rest
        o_ref[...] = res_ref[...] + gate_ref[...] * (acc[:, :tn] * jax.nn.sigmoid(acc[:, tn:]))
    else:
        raise ValueError(kind)


def _mm(kind, a, w, lead, col_blocks, tn, n_out, out_dtype, *, k_part=(0, 1), coef=1.0,
        res=None, gate=None, tm=None, side=None, w_t=False):
    m = a.shape[0]
    kp = w.shape[-1 if w_t else -2] // k_part[1]
    n_w = len(col_blocks)
    assert not (w_t and n_w > 1)
    if tm is None:
        tm = _pick(m, ROW_TILES)
    n_i, n_j = m // tm, n_out // tn
    n_lead = len(lead)
    in_specs = [pl.BlockSpec((tm, kp), lambda i, j: (i, k_part[0]))]
    if w_t:
        in_specs += [pl.BlockSpec((None,) * n_lead + (tn, kp),
                                  lambda i, j, cb=cb: tuple(lead) + (cb(j), k_part[0]))
                     for cb in col_blocks]
    else:
        in_specs += [pl.BlockSpec((None,) * n_lead + (kp, tn),
                                  lambda i, j, cb=cb: tuple(lead) + (k_part[0], cb(j)))
                     for cb in col_blocks]
    args = [a] + [w] * n_w
    out_specs = [pl.BlockSpec((tm, tn), lambda i, j: (i, j))]
    out_shape = [jax.ShapeDtypeStruct((m, n_out), out_dtype)]
    side_bytes = 0
    if side is not None:
        s_arr, s_lead = side
        s_rows, s_cols = s_arr.shape[-2:]
        assert s_rows % n_j == 0 and s_cols % n_i == 0
        sblk = (s_rows // n_j, s_cols // n_i)
        in_specs.append(pl.BlockSpec((None,) * len(s_lead) + sblk, lambda i, j: tuple(s_lead) + (j, i)))
        args.append(s_arr)
        out_specs.append(pl.BlockSpec(sblk, lambda i, j: (j, i)))
        out_shape.append(jax.ShapeDtypeStruct((s_rows, s_cols), BF16))
        side_bytes = 2 * (_nbytes(sblk, F32) + _nbytes(sblk, BF16))
    if res is not None:
        mod, rows, l, slot = gate
        in_specs += [pl.BlockSpec((tm, tn), lambda i, j: (i, j)), _mod_spec(rows, tm, l, slot, tn)]
        args += [res, mod]
    est = (2 * (_nbytes((tm, kp), BF16) + n_w * _nbytes((kp, tn), w.dtype) + _nbytes((tm, tn), out_dtype))
           + n_w * _nbytes((kp, tn), BF16) + 3 * n_w * _nbytes((tm, tn), F32)
           + (4 * _nbytes((tm, tn), F32) if res is not None else 0) + side_bytes + (2 << 20))
    outs = pl.pallas_call(
        functools.partial(_mm_body, kind, coef, n_w, side is not None, w_t),
        grid=(n_i, n_j),
        in_specs=in_specs,
        out_specs=out_specs,
        out_shape=out_shape,
        compiler_params=_cparams(2, est),
        name="mm_" + kind,
    )(*args)
    return outs if side is not None else outs[0]


def _side_cast_row_tile(m, n_cols):
    for tm in (1280, 1024, 640, 512, 256, 128, 64, 32, 16):
        if m % tm == 0 and n_cols % (m // tm) == 0 and (n_cols // (m // tm)) % LANES == 0:
            return tm
    return None


def _conv_body(grid_rows, n_q_blocks, q_scale, x_ref, w_ref, b_ref, o_ref):
    x = x_ref[...]
    t_len = x.shape[0]
    t_idx = lax.broadcasted_iota(jnp.int32, x.shape, 0)
    if grid_rows is None:
        taps = [(0, dc) for dc in (-1, 0, 1)]
        col = t_idx
        n_cols = t_len
        row = jnp.zeros_like(t_idx)
        n_rows = 1
    else:
        taps = [(dr, dc) for dr in (-1, 0, 1) for dc in (-1, 0, 1)]
        col = t_idx % GRID_W
        n_cols = GRID_W
        row = t_idx // GRID_W
        n_rows = grid_rows
    acc = jnp.zeros(x.shape, F32)
    for dr, dc in taps:
        off = dr * n_cols + dc
        xs = x if off == 0 else pltpu.roll(x, shift=(-off) % t_len, axis=0)
        ok = ((col + dc >= 0) & (col + dc < n_cols) & (row + dr >= 0) & (row + dr < n_rows))
        wv = w_ref[dr + 1, dc + 1:dc + 2, :]
        acc = acc + jnp.where(ok, xs, 0.0) * wv
    y = acc + b_ref[...]
    y = y * jax.nn.sigmoid(y)
    scale = jnp.where(pl.program_id(1) < n_q_blocks, q_scale, 1.0)
    o_ref[...] = (y * scale).astype(o_ref.dtype)


def _qk_conv(proj, conv_w, conv_b, row_off, n_seq, seq_len, grid_rows, dk):
    ch = conv_w.shape[-1]
    tc = _pick(ch // 2, (512, 256, 128))
    n_q_blocks = (ch // 2) // tc
    body = functools.partial(_conv_body, grid_rows, n_q_blocks, float(dk) ** -0.5)
    est = 2 * (_nbytes((seq_len, tc), F32) + _nbytes((seq_len, tc), BF16)) + 8 * _nbytes((seq_len, tc), F32)
    return pl.pallas_call(
        body,
        grid=(n_seq, ch // tc),
        in_specs=[pl.BlockSpec((seq_len, tc), lambda b, c: (row_off // seq_len + b, c)),
                  pl.BlockSpec((3, 3, tc), lambda b, c: (0, 0, c)),
                  pl.BlockSpec((1, tc), lambda b, c: (0, c))],
        out_specs=pl.BlockSpec((seq_len, tc), lambda b, c: (b, c)),
        out_shape=jax.ShapeDtypeStruct((n_seq * seq_len, ch), BF16),
        compiler_params=_cparams(2, est),
        name="qk_conv",
    )(proj, conv_w, conv_b.reshape(1, ch))


def _log_sigmoid(x):
    return jnp.minimum(x, 0.0) - jnp.log1p(jnp.exp(-jnp.abs(x)))


def _mlstm_decay(backward, li_col, fp_col, li_row, fp_row, m_prev):
    l = li_col.shape[0]
    lf_col = _log_sigmoid(fp_col)
    lf_row = _log_sigmoid(fp_row)
    t_i = lax.broadcasted_iota(jnp.int32, (l, l), 0)
    s_i = lax.broadcasted_iota(jnp.int32, (l, l), 1)
    causal = (s_i >= t_i) if backward else (s_i <= t_i)
    causal_t = (t_i >= s_i) if backward else (t_i <= s_i)
    b_col = jnp.sum(jnp.where(causal, lf_row, 0.0), axis=1, keepdims=True)
    b_row = jnp.sum(jnp.where(causal_t, lf_col, 0.0), axis=0, keepdims=True)
    b_last = jnp.sum(lf_col, axis=0, keepdims=True)
    log_d = jnp.where(causal, b_col - b_row + li_row, -jnp.inf)
    log_inter = b_col + m_prev
    m_out = jnp.maximum(log_inter, jnp.max(log_d, axis=1, keepdims=True))
    dmat = jnp.exp(log_d - m_out)
    s_inter = jnp.exp(log_inter - m_out)
    log_w = b_last - b_col + li_col
    m_new = jnp.maximum(b_last + m_prev, jnp.max(log_w, axis=0, keepdims=True))
    w = jnp.exp(log_w - m_new)
    decay = jnp.exp(b_last + m_prev - m_new)
    return dmat, s_inter, jnp.exp(-m_out), w, decay, m_new


def _mlstm_chunk(chains):
    qk = [lax.dot_general(ch["q"], ch["k"], (((1,), (1,)), ((), ())), preferred_element_type=F32)
          for ch in chains]
    qc = [None if ch["c"] is None else _dot(ch["q"], ch["c"].astype(BF16)) for ch in chains]
    dec = [_mlstm_decay(ch["backward"], ch["li_col"], ch["fp_col"], ch["li_row"], ch["fp_row"], ch["m"])
           for ch in chains]
    s, den, kw, n_new = [], [], [], []
    for ch, (dmat, s_inter, _, w, decay, _), qk_i in zip(chains, dec, qk):
        s_i = qk_i * dmat
        s.append(s_i)
        den_i = jnp.sum(s_i, axis=1, keepdims=True)
        kw_i = ch["k"].astype(F32) * w
        n_i = jnp.sum(kw_i, axis=0, keepdims=True)
        if ch["n"] is not None:
            den_i = den_i + s_inter * jnp.sum(ch["q"].astype(F32) * ch["n"], axis=1, keepdims=True)
            n_i = decay * ch["n"] + n_i
        den.append(den_i)
        kw.append(kw_i)
        n_new.append(n_i)
    sv = [_dot(s_i.astype(BF16), ch["v"]) for ch, s_i in zip(chains, s)]
    kv = [lax.dot_general(kw_i.astype(BF16), ch["v"], (((0,), (0,)), ((), ())), preferred_element_type=F32)
          for ch, kw_i in zip(chains, kw)]
    out = []
    for ch, (_, s_inter, exp_neg_m, _, decay, m_new), qc_i, den_i, sv_i, kv_i, n_i in zip(
            chains, dec, qc, den, sv, kv, n_new):
        num = sv_i if qc_i is None else sv_i + s_inter * qc_i
        h = num / jnp.maximum(jnp.abs(den_i), exp_neg_m)
        out.append((h, kv_i if ch["c"] is None else decay * ch["c"] + kv_i, n_i, m_new))
    return out


def _mlstm_body(zero_init, emit_state, single_chunk, n_heads, dk, dv, *refs):
    it = iter(refs)
    fwd_in = tuple(next(it) for _ in range(5))
    bwd_in = tuple(next(it) for _ in range(5))
    if not zero_init:
        c0, n0, m0 = next(it), next(it), next(it)
    if emit_state:
        next(it), next(it), next(it)
    hf_o, hb_o = next(it), next(it)
    if emit_state:
        c_o, n_o, m_o = next(it), next(it), next(it)
    carried = not (zero_init and single_chunk)
    if carried:
        c_s, n_s, m_s = next(it), next(it), next(it)
        c_idx = pl.program_id(2)

        @pl.when(c_idx == 0)
        def _():
            if zero_init:
                c_s[...] = jnp.zeros(c_s.shape, F32)
                n_s[...] = jnp.zeros(n_s.shape, F32)
                m_s[...] = jnp.zeros(m_s.shape, F32)
            else:
                c_s[...] = c0[...]
                n_s[...] = n0[...]
                m_s[...] = m0[...]

        n_all, m_all = n_s[...], m_s[...]
    else:
        m_all = jnp.zeros((2, n_heads), F32)
    head_row = lax.broadcasted_iota(jnp.int32, (n_heads, dk), 0)
    m_dir = lax.broadcasted_iota(jnp.int32, m_all.shape, 0)
    m_head = lax.broadcasted_iota(jnp.int32, m_all.shape, 1)
    chains = []
    for d, (q_ref, k_ref, v_ref, gc_ref, gr_ref) in enumerate((fwd_in, bwd_in)):
        for h in range(n_heads):
            gi, gf = (2 * d) * n_heads + h, (2 * d + 1) * n_heads + h
            chains.append(dict(
                backward=(d == 1),
                q=q_ref[:, h * dk:(h + 1) * dk], k=k_ref[:, h * dk:(h + 1) * dk],
                v=v_ref[:, h * dv:(h + 1) * dv],
                li_col=gc_ref[:, gi:gi + 1], fp_col=gc_ref[:, gf:gf + 1],
                li_row=gr_ref[gi:gi + 1, :], fp_row=gr_ref[gf:gf + 1, :],
                c=c_s[d, h] if carried else None, n=n_all[d, h:h + 1, :] if carried else None,
                m=m_all[d:d + 1, h:h + 1]))
    results = _mlstm_chunk(chains)
    m_next = m_all
    n_next = []
    for d, h_o in enumerate((hf_o, hb_o)):
        n_d = jnp.zeros((n_heads, dk), F32)
        for h in range(n_heads):
            hh, c_new, n_new, m_new = results[d * n_heads + h]
            h_o[:, h * dv:(h + 1) * dv] = hh
            if carried:
                c_s[d, h] = c_new
            else:
                c_o[d, h] = c_new
            n_d = jnp.where(head_row == h, n_new, n_d)
            m_next = jnp.where((m_dir == d) & (m_head == h), m_new, m_next)
        n_next.append(n_d)
    if carried:
        for d in range(2):
            n_s[d] = n_next[d]
        m_s[...] = m_next
        if emit_state:
            @pl.when(c_idx == pl.num_programs(2) - 1)
            def _():
                c_o[...] = c_s[...]
                n_o[...] = n_s[...]
                m_o[...] = m_s[...]
    else:
        for d in range(2):
            n_o[d] = n_next[d]
        m_o[...] = m_next


def _mlstm_scan(qk, v, gates, row_off, n_seq, seq_len, dk, dv, layer_j, state_in=None, state_out=None):
    nh = M_HEADS
    hb = _pick(nh, (M_HEADS_PER_STEP, 2, 1))
    n_hg = nh // hb
    lc = _pick(seq_len, (M_SCAN_CHUNK, 128, 64, 32, 16, 8))
    nc = seq_len // lc
    n_rows = n_seq * seq_len
    zero_init = state_in is None
    emit_state = state_out is not None
    carried = not (zero_init and nc == 1)
    assert carried or emit_state
    n_tot = gates.shape[0]
    g5 = gates.reshape(n_tot // lc, lc, M_GATES, n_hg, hb)
    g_col = jnp.transpose(g5, (3, 0, 1, 2, 4)).reshape(n_hg, n_tot // lc, lc, M_GATES * hb)
    g_row = jnp.swapaxes(g_col, 2, 3)
    off_c = row_off // lc

    def specs(cmap):
        return [pl.BlockSpec((lc, hb * dk), lambda b, g, c: (cmap(b, c), g)),
                pl.BlockSpec((lc, hb * dk), lambda b, g, c: (cmap(b, c), n_hg + g)),
                pl.BlockSpec((lc, hb * dv), lambda b, g, c: (off_c + cmap(b, c), g)),
                pl.BlockSpec((None, None, lc, M_GATES * hb), lambda b, g, c: (g, off_c + cmap(b, c), 0, 0)),
                pl.BlockSpec((None, None, M_GATES * hb, lc), lambda b, g, c: (g, off_c + cmap(b, c), 0, 0))]

    def fwd(b, c):
        return b * nc + c

    def bwd(b, c):
        return b * nc + (nc - 1 - c)

    def group_views(state):
        c_a, n_a, m_a = state
        lead = c_a.shape[:3]
        return (c_a.reshape(lead + (n_hg, hb, dk, dv)), n_a.reshape(lead + (n_hg, hb, dk)),
                jnp.swapaxes(m_a.reshape(lead + (n_hg, hb)), 2, 3))

    def state_specs():
        return [pl.BlockSpec((None, None, 2, None, hb, dk, dv), lambda b, g, c: (b, layer_j, 0, g, 0, 0, 0)),
                pl.BlockSpec((None, None, 2, None, hb, dk), lambda b, g, c: (b, layer_j, 0, g, 0, 0)),
                pl.BlockSpec((None, None, None, 2, hb), lambda b, g, c: (b, layer_j, g, 0, 0))]

    in_specs = specs(fwd) + specs(bwd)
    args = [qk, qk, v, g_col, g_row] * 2
    if not zero_init:
        in_specs += state_specs()
        args += list(group_views(state_in))
    aliases = {}
    if emit_state:
        out_views = group_views(state_out)
        for t, s in enumerate(out_views):
            aliases[len(args)] = 2 + t
            in_specs.append(pl.BlockSpec(memory_space=pl.ANY))
            args.append(s)
    out_specs = [pl.BlockSpec((lc, hb * dv), lambda b, g, c: (fwd(b, c), g)),
                 pl.BlockSpec((lc, hb * dv), lambda b, g, c: (bwd(b, c), g))]
    out_shape = [jax.ShapeDtypeStruct((n_rows, nh * dv), F32)] * 2
    if emit_state:
        out_specs += state_specs()
        out_shape += [jax.ShapeDtypeStruct(s.shape, F32) for s in out_views]
    scratch = []
    if carried:
        scratch = [pltpu.VMEM((2, hb, dk, dv), F32), pltpu.VMEM((2, hb, dk), F32), pltpu.VMEM((2, hb), F32)]
    state_bytes = _nbytes((2, hb, dk, dv), F32)
    chain_bytes = 4 * _nbytes((lc, lc), F32) + 4 * _nbytes((lc, dv), F32) + 2 * _nbytes((dk, dv), F32)
    est = (state_bytes * ((1 if carried else 0) + (2 if not zero_init else 0) + (2 if emit_state else 0))
           + 2 * hb * chain_bytes + 8 * _nbytes((lc, hb * dv), F32) + 16 * _nbytes((lc, hb * dk), BF16)
           + (4 << 20))
    outs = pl.pallas_call(
        functools.partial(_mlstm_body, zero_init, emit_state, nc == 1, hb, dk, dv),
        grid=(n_seq, n_hg, nc),
        in_specs=in_specs,
        out_specs=out_specs,
        out_shape=out_shape,
        scratch_shapes=scratch,
        input_output_aliases=aliases,
        compiler_params=_cparams(3, est),
        name="mlstm_scan",
    )(*args)
    new_state = None
    if emit_state:
        c_a, n_a, m_a = state_out
        new_state = (outs[2].reshape(c_a.shape), outs[3].reshape(n_a.shape),
                     jnp.swapaxes(outs[4], 2, 3).reshape(m_a.shape))
    return outs[0], outs[1], new_state


def _mlstm_combine_body(hf_ref, hb_ref, o_ref, g_ref, z_ref):
    hh = _norm_rows(hf_ref[...] + hb_ref[...], g_ref[...])
    z_ref[...] = (hh * jax.nn.sigmoid(o_ref[...])).astype(z_ref.dtype)


def _mlstm_combine(h_f, h_b, o_gate, head_g, dv, row_off):
    n_rows, width = h_f.shape
    tm = _pick(math.gcd(n_rows, row_off) if row_off else n_rows, (512, 256, 128, 64, 32, 16, 8))
    est = 2 * (3 * _nbytes((tm, dv), F32) + _nbytes((tm, dv), BF16)) + 4 * _nbytes((tm, dv), F32)
    return pl.pallas_call(
        _mlstm_combine_body,
        grid=(n_rows // tm, M_HEADS),
        in_specs=[pl.BlockSpec((tm, dv), lambda i, h: (i, h)),
                  pl.BlockSpec((tm, dv), lambda i, h: (i, h)),
                  pl.BlockSpec((tm, dv), lambda i, h: (row_off // tm + i, h)),
                  pl.BlockSpec((1, dv), lambda i, h: (0, h))],
        out_specs=pl.BlockSpec((tm, dv), lambda i, h: (i, h)),
        out_shape=jax.ShapeDtypeStruct((n_rows, width), BF16),
        compiler_params=_cparams(2, est),
        name="mlstm_combine",
    )(h_f, h_b, o_gate, head_g.reshape(1, width))


def _s5_body(zero_init, emit_y, emit_state, seq_len, tc_len, *refs):
    it = iter(refs)
    u_ref, bw_ref, ar_ref, ai_ref = (next(it) for _ in range(4))
    cw_ref = next(it) if emit_y else None
    if not zero_init:
        x0r_ref, x0i_ref = next(it), next(it)
    y_ref = next(it) if emit_y else None
    if emit_state:
        fr_ref, fi_ref = next(it), next(it)
    e_s = next(it)
    xb_s = next(it) if emit_y else None
    ns = ar_ref.shape[-1]
    cin = u_ref.shape[-1]
    n_tc = seq_len // tc_len
    rows_c = tc_len * S5_ROWS
    pair_rows = 2 * S5_ROWS

    for d in range(2):
        ar = jnp.broadcast_to(ar_ref[d], (S5_ROWS, ns))
        ai = jnp.broadcast_to(ai_ref[d], (S5_ROWS, ns))

        def step(r, xr, xi):
            er = e_s[pl.ds(r, S5_ROWS), 0:ns]
            ei = e_s[pl.ds(r, S5_ROWS), ns:2 * ns]
            return ar * xr - ai * xi + er, ar * xi + ai * xr + ei

        if zero_init:
            xr = xi = jnp.zeros((S5_ROWS, ns), F32)
        else:
            xr, xi = x0r_ref[d], x0i_ref[d]
        for c in (range(n_tc) if d == 0 else range(n_tc - 1, -1, -1)):
            u2 = u_ref[c * tc_len:(c + 1) * tc_len].reshape(rows_c, cin).astype(BF16)
            e_s[...] = _dot(u2, bw_ref[d])

            def pair(p, carry, d=d, c=c):
                q = p if d == 0 else tc_len // 2 - 1 - p
                r_lo = pl.multiple_of(q * pair_rows, pair_rows)
                r_hi = r_lo + S5_ROWS
                x1 = step(r_lo if d == 0 else r_hi, *carry)
                x2 = step(r_hi if d == 0 else r_lo, *x1)
                if emit_y:
                    lo, hi = (x1, x2) if d == 0 else (x2, x1)
                    r_out = pl.multiple_of(c * rows_c + r_lo, pair_rows)
                    for part in range(2):
                        col = (2 * d + part) * ns
                        xb_s[pl.ds(r_out, pair_rows), col:col + ns] = (
                            jnp.concatenate([lo[part], hi[part]], axis=0).astype(BF16))
                return x2

            xr, xi = lax.fori_loop(0, tc_len // 2, pair, (xr, xi), unroll=4)
        if emit_state:
            fr_ref[d] = xr
            fi_ref[d] = xi

    if emit_y:
        y = _dot(xb_s[...], cw_ref[...])
        y_ref[...] = y.reshape(y_ref.shape)


def _s5_scan(u_tm, bw, a_re, a_im, cw, state, emit_y, emit_state):
    seq_len, n_pad, e_width = u_tm.shape
    n_sb = bw.shape[1]
    cin = e_width // n_sb
    ns = bw.shape[-1] // 2
    tc_len = _pick(seq_len, (128, 64, 32, 16, 8, 4, 2))
    n_bg = n_pad // S5_ROWS
    zero_init = state is None

    def wspec(r, c):
        return pl.BlockSpec((2, None, r, c), lambda bg, sb: (0, sb, 0, 0))

    def sspec():
        return pl.BlockSpec((2, S5_ROWS, ns), lambda bg, sb: (0, bg, sb))

    in_specs = [pl.BlockSpec((seq_len, S5_ROWS, cin), lambda bg, sb: (0, bg, sb)),
                wspec(cin, 2 * ns), wspec(1, ns), wspec(1, ns)]
    args = [u_tm, bw, a_re, a_im]
    if emit_y:
        in_specs.append(pl.BlockSpec((None, 4 * ns, cin), lambda bg, sb: (sb, 0, 0)))
        args.append(cw)
    if not zero_init:
        in_specs += [sspec(), sspec()]
        args += list(state)
    out_specs, out_shape = [], []
    scratch = [pltpu.VMEM((tc_len * S5_ROWS, 2 * ns), F32)]
    if emit_y:
        out_specs.append(pl.BlockSpec((seq_len, S5_ROWS, cin), lambda bg, sb: (0, bg, sb)))
        out_shape.append(jax.ShapeDtypeStruct((seq_len, n_pad, e_width), F32))
        scratch.append(pltpu.VMEM((seq_len * S5_ROWS, 4 * ns), BF16))
    if emit_state:
        out_specs += [sspec(), sspec()]
        out_shape += [jax.ShapeDtypeStruct((2, n_pad, n_sb * ns), F32)] * 2
    est = (2 * _nbytes((tc_len * S5_ROWS, 2 * ns), F32) + _nbytes((seq_len * S5_ROWS, 4 * ns), BF16)
           + 5 * _nbytes((seq_len, S5_ROWS, cin), F32) + 8 * _nbytes((cin, 2 * ns), BF16) + (4 << 20))
    outs = pl.pallas_call(
        functools.partial(_s5_body, zero_init, emit_y, emit_state, seq_len, tc_len),
        grid=(n_bg, n_sb),
        in_specs=in_specs,
        out_specs=out_specs,
        out_shape=out_shape,
        scratch_shapes=scratch,
        compiler_params=_cparams(2, est),
        name="s5_scan",
    )(*args)
    y = outs[0] if emit_y else None
    fin = tuple(outs[-2:]) if emit_state else None
    return y, fin


def _s5_carry_body(n_seq, n_seg, fr_ref, fi_ref, x0r_ref, x0i_ref, pr_ref, pi_ref, or_ref, oi_ref):
    or_ref[...] = jnp.zeros(or_ref.shape, F32)
    oi_ref[...] = jnp.zeros(oi_ref.shape, F32)
    for d in range(2):
        pr, pi = pr_ref[d], pi_ref[d]
        order = range(n_seg) if d == 0 else range(n_seg - 1, -1, -1)
        for b in range(n_seq):
            cr, ci = x0r_ref[d, b:b + 1, :], x0i_ref[d, b:b + 1, :]
            for k in order:
                r = b * n_seg + k
                or_ref[d, r:r + 1, :] = cr
                oi_ref[d, r:r + 1, :] = ci
                fr, fi = fr_ref[d, r:r + 1, :], fi_ref[d, r:r + 1, :]
                cr, ci = pr * cr - pi * ci + fr, pr * ci + pi * cr + fi


def _s5_carry(fin, x0, a_pow, n_seq, n_seg):
    shape = fin[0].shape
    return pl.pallas_call(
        functools.partial(_s5_carry_body, n_seq, n_seg),
        out_shape=[jax.ShapeDtypeStruct(shape, F32)] * 2,
        compiler_params=_cparams(0, 16 * _nbytes(shape, F32)),
        name="s5_carry",
    )(fin[0], fin[1], x0[0], x0[1], a_pow[0], a_pow[1])


def _s5_gelu_body(y_ref, u_ref, d_ref, z_ref, v_s):
    c = math.sqrt(2.0 / math.pi)
    v_s[...] = y_ref[...] + d_ref[...] * u_ref[...]
    for b in range(S5_ROWS):
        v = v_s[:, b, :]
        z = 0.5 * v * (1.0 + jnp.tanh(c * (v + 0.044715 * (v * v * v))))
        z_ref[b] = z.astype(z_ref.dtype)


def _s5_gelu(y, u_tm, d_skip):
    seq_len, n_pad, e_width = u_tm.shape
    tt = _pick(seq_len, (128, 64, 32, 16, 8))
    ec = _pick(e_width, (512, 256, 128))
    blk = (tt, S5_ROWS, ec)
    est = 2 * (2 * _nbytes(blk, F32) + _nbytes(blk, BF16)) + 8 * _nbytes(blk, F32)
    return pl.pallas_call(
        _s5_gelu_body,
        grid=(n_pad // S5_ROWS, seq_len // tt, e_width // ec),
        in_specs=[pl.BlockSpec(blk, lambda bg, t, e: (t, bg, e)),
                  pl.BlockSpec(blk, lambda bg, t, e: (t, bg, e)),
                  pl.BlockSpec((1, ec), lambda bg, t, e: (0, e))],
        out_specs=pl.BlockSpec((S5_ROWS, tt, ec), lambda bg, t, e: (bg, t, e)),
        out_shape=jax.ShapeDtypeStruct((n_pad, seq_len, e_width), BF16),
        scratch_shapes=[pltpu.VMEM(blk, F32)],
        compiler_params=_cparams(3, est),
        name="s5_gelu",
    )(y, u_tm, d_skip.reshape(1, e_width))


def _s5_weights(lam_re, lam_im, log_dt, b_re, b_im, c_re, c_im, seg_len):
    lr = jnp.minimum(lam_re.astype(F32), -1e-4)
    li = lam_im.astype(F32)
    dt = jnp.exp(log_dt.astype(F32))[..., None]
    mag = jnp.exp(lr * dt)
    ar = mag * jnp.cos(li * dt)
    ai = mag * jnp.sin(li * dt)
    den = lr * lr + li * li
    xr = ar - 1.0
    cr = (xr * lr + ai * li) / den
    ci = (ai * lr - xr * li) / den
    bp_re = cr[..., None] * b_re - ci[..., None] * b_im
    bp_im = cr[..., None] * b_im + ci[..., None] * b_re
    n_dir, g, p = lam_re.shape
    gc = b_re.shape[-1]
    sbg = min(S5_SB_GROUPS, g)
    n_sb = g // sbg
    same_group = (lax.broadcasted_iota(jnp.int32, (sbg * gc, sbg * p), 0) // gc
                  == lax.broadcasted_iota(jnp.int32, (sbg * gc, sbg * p), 1) // p)

    def bd_in(bp):
        t = jnp.transpose(bp.reshape(n_dir, n_sb, sbg, p, gc), (0, 1, 4, 2, 3))
        t = jnp.tile(t.reshape(n_dir, n_sb, gc, sbg * p), (1, 1, sbg, 1))
        return jnp.where(same_group, t, 0.0)

    def bd_out(cc):
        t = jnp.transpose(cc.reshape(n_dir, n_sb, sbg, gc, p), (0, 1, 2, 4, 3))
        t = jnp.tile(t.reshape(n_dir, n_sb, sbg * p, gc), (1, 1, 1, sbg))
        return jnp.where(same_group.T, t, 0.0)

    bw = jnp.concatenate([bd_in(bp_re), bd_in(bp_im)], axis=-1).astype(BF16)
    cw = jnp.concatenate([bd_out(c_re.astype(F32)), -bd_out(c_im.astype(F32))], axis=-2).astype(BF16)
    cw = jnp.swapaxes(cw, 0, 1).reshape(n_sb, n_dir * 2 * sbg * p, sbg * gc)
    a_re = ar.reshape(n_dir, n_sb, 1, sbg * p)
    a_im = ai.reshape(n_dir, n_sb, 1, sbg * p)
    pr, pi = jnp.ones_like(ar), jnp.zeros_like(ai)
    br, bi = ar, ai
    e = seg_len
    while e:
        if e & 1:
            pr, pi = pr * br - pi * bi, pr * bi + pi * br
        br, bi = br * br - bi * bi, 2.0 * br * bi
        e >>= 1
    a_pow = (pr.reshape(n_dir, 1, g * p), pi.reshape(n_dir, 1, g * p))
    return bw, a_re, a_im, cw, a_pow


def _s5_group(x, g, mod, l, w_in, lead, weights, d_skip, row_off, n_seq, seq_len, cond0, per_seq_cond,
              state, want_state):
    bw, a_re, a_im, cw, a_pow = weights
    d = x.shape[1]
    seg = _pick(seq_len, (S5_SEG, 128, 64, 32, 16, 8))
    n_seg = seq_len // seg
    n_ps = n_seq * n_seg
    n_pad = -(-n_ps // S5_ROWS) * S5_ROWS

    def cond_of(p):
        return cond0 + (p // n_seg if per_seq_cond else 0)

    h_tm = _norm_mod_tm(x, g, mod, l, 1, row_off, n_ps, seg, n_pad, cond_of)
    e_w = w_in.shape[-1]
    u_tm = _mm("plain", h_tm.reshape(seg * n_pad, d), w_in, lead, [lambda j: j],
               _pick(e_w, (512, 256, 128)), e_w, F32).reshape(seg, n_pad, e_w)

    def pad_rows(s):
        return jnp.pad(s, ((0, 0), (0, n_pad - s.shape[1]), (0, 0)))

    if n_seg == 1:
        start = None if state is None else tuple(pad_rows(s) for s in state)
    else:
        assert not want_state
        _, fin = _s5_scan(u_tm, bw, a_re, a_im, cw, None, emit_y=False, emit_state=True)
        x0 = state if state is not None else tuple(jnp.zeros((2, n_seq, fin[0].shape[-1]), F32) for _ in range(2))
        start = _s5_carry(fin, x0, a_pow, n_seq, n_seg)
    y, fin = _s5_scan(u_tm, bw, a_re, a_im, cw, start, emit_y=True, emit_state=want_state)
    z = _s5_gelu(y, u_tm, d_skip).reshape(n_pad * seg, e_w)[:n_seq * seq_len]
    if want_state:
        fin = tuple(f[:, :n_seq] for f in fin)
    return z, fin


def kernel(x_prompt, x_sample, state_mlstm_C, state_mlstm_n, state_mlstm_m, state_s5_re, state_s5_im, c, c_ctx, w_ada, b_ada, norm_g, final_g, w_ffn_in, w_ffn_out, m_w_in, m_conv_w, m_conv_b, m_gate_b, m_head_g, m_w_out, s5_w_in, s5_lam_re, s5_lam_im, s5_log_dt, s5_b_re, s5_b_im, s5_c_re, s5_c_im, s5_d, s5_w_out):
    bp, tp, d = x_prompt.shape
    bs, ts, _ = x_sample.shape
    depth = w_ada.shape[0]
    n_ctx, n_lat = bp * tp, bs * ts
    rows = _Rows(n_ctx, bs, ts)
    tm = rows.tile()
    x = jnp.concatenate([x_prompt.reshape(n_ctx, d), x_sample.reshape(n_lat, d)], axis=0)

    cond = jnp.concatenate([c_ctx[None, :], c], axis=0).astype(F32)
    mod = _ada(cond, w_ada, b_ada).reshape(depth, 1 + bs, N_SUB * N_MOD, 1, d)

    n_ml, n_s5 = m_w_in.shape[0], s5_w_in.shape[0]
    dk, dv = state_mlstm_C.shape[-2], state_mlstm_C.shape[-1]
    qk_w = 2 * M_HEADS * dk
    v_w = M_HEADS * dv
    main_w = qk_w + 2 * v_w
    m_w_t = jnp.swapaxes(m_w_in, 1, 2)
    g_groups, p_state = state_s5_re.shape[-2], state_s5_re.shape[-1]
    d_ff = w_ffn_out.shape[-2]

    def gate_of(l, sub):
        return (mod, rows, l, sub * N_MOD + 2)

    def ffn(x, l, idx, sub):
        h = _norm_mod(x, norm_g[l, sub], mod, rows, l, sub)
        tn = _pick(d_ff, (256, 128))
        nf = d_ff // tn
        up_cols = [lambda j: j, lambda j: nf + j]
        tm_side = _side_cast_row_tile(rows.n, d)
        if tm_side is not None:
            a, w_out = _mm("swiglu", h, w_ffn_in, (l, idx), up_cols, tn, d_ff, BF16, tm=tm_side,
                           side=(w_ffn_out, (l, idx)))
            w_lead, tn_out = (), _pick(d, (512, 256, 128))
        else:
            a = _mm("swiglu", h, w_ffn_in, (l, idx), up_cols, tn, d_ff, BF16)
            w_out, w_lead, tn_out = w_ffn_out, (l, idx), _pick(d, (256, 128))
        k_parts = 2 if d_ff % (2 * LANES) == 0 and d_ff > 4096 else 1
        for p in range(k_parts):
            x = _mm("resid", a, w_out, w_lead, [lambda j: j], tn_out, d, F32,
                    k_part=(p, k_parts), coef=0.5, res=x, gate=gate_of(l, sub), tm=tm)
        return x

    def mlstm(x, l, j, new_state):
        h = _norm_mod(x, norm_g[l, 1], mod, rows, l, 1)
        tn = _pick(math.gcd(qk_w, v_w), (512, 256, 128))
        qk_pre = _mm("plain", h, m_w_t, (j,), [lambda jb: jb], tn, qk_w, F32, w_t=True)
        v = _mm("plain", h, m_w_t, (j,), [lambda jb: qk_w // tn + jb], tn, v_w, BF16, w_t=True)
        o_gate = _mm("plain", h, m_w_t, (j,), [lambda jb: (qk_w + v_w) // tn + jb], tn, v_w, F32, w_t=True)
        gates = _mm("plain", h, m_w_t, (j,), [lambda jb: main_w // LANES + jb], LANES, LANES, F32,
                    w_t=True)[:, :M_GATES * M_HEADS]
        gates = gates + m_gate_b[j].reshape(1, M_GATES * M_HEADS).astype(F32)
        qk_p = _qk_conv(qk_pre, m_conv_w[j], m_conv_b[j], 0, bp, tp, None, dk)
        qk_s = _qk_conv(qk_pre, m_conv_w[j], m_conv_b[j], n_ctx, bs, ts, ts // GRID_W, dk)
        hf_p, hb_p, new_state = _mlstm_scan(qk_p, v, gates, 0, bp, tp, dk, dv, j, None, new_state)
        cached = (state_mlstm_C, state_mlstm_n, state_mlstm_m)
        hf_s, hb_s, _ = _mlstm_scan(qk_s, v, gates, n_ctx, bs, ts, dk, dv, j, cached, None)
        z = jnp.concatenate([_mlstm_combine(hf_p, hb_p, o_gate, m_head_g[j], dv, 0),
                             _mlstm_combine(hf_s, hb_s, o_gate, m_head_g[j], dv, n_ctx)], axis=0)
        x = _mm("resid", z, m_w_out, (j,), [lambda jb: jb], _pick(d, (512, 256, 128)), d, F32,
                res=x, gate=gate_of(l, 1), tm=tm)
        return x, new_state

    def s5(x, l, j):
        seg = _pick(ts, (S5_SEG, 128, 64, 32, 16, 8))
        weights = _s5_weights(s5_lam_re[j], s5_lam_im[j], s5_log_dt[j], s5_b_re[j], s5_b_im[j],
                              s5_c_re[j], s5_c_im[j], seg)
        cached = tuple(jnp.swapaxes(s[:, j], 0, 1).reshape(2, bs, g_groups * p_state).astype(F32)
                       for s in (state_s5_re, state_s5_im))
        z_p, fin = _s5_group(x, norm_g[l, 1], mod, l, s5_w_in, (j,), weights, s5_d[j],
                             0, bp, tp, 0, False, None, True)
        z_s, _ = _s5_group(x, norm_g[l, 1], mod, l, s5_w_in, (j,), weights, s5_d[j],
                           n_ctx, bs, ts, 1, True, cached, False)
        z = jnp.concatenate([z_p, z_s], axis=0)
        tn = _pick(d, (256, 128))
        nd = d // tn
        x = _mm("glu_resid", z, s5_w_out, (j,), [lambda jb: jb, lambda jb: nd + jb], tn, d, F32,
                res=x, gate=gate_of(l, 1), tm=tm)
        fin = tuple(jnp.swapaxes(f, 0, 1).reshape(bp, 2, g_groups, p_state) for f in fin)
        return x, fin

    new_ml = (jnp.zeros((bp, n_ml, 2, M_HEADS, dk, dv), F32), jnp.zeros((bp, n_ml, 2, M_HEADS, dk), F32),
              jnp.zeros((bp, n_ml, 2, M_HEADS), F32))
    new_re, new_im = [], []
    for l in range(depth):
        x = ffn(x, l, 0, 0)
        if l % 2 == 0:
            x, new_ml = mlstm(x, l, l // 2, new_ml)
        else:
            x, (f_re, f_im) = s5(x, l, l // 2)
            new_re.append(f_re)
            new_im.append(f_im)
        x = ffn(x, l, 1, 2)

    y_prompt = _final_norm(x, final_g, 0, n_ctx).reshape(bp, tp, d)
    y_sample = _final_norm(x, final_g, n_ctx, n_lat).reshape(bs, ts, d)
    return (y_prompt, y_sample) + tuple(new_ml) + (jnp.stack(new_re, axis=1), jnp.stack(new_im, axis=1))
```

```python
import functools
import math

import jax
import jax.numpy as jnp
from jax import lax
from jax.experimental import pallas as pl
from jax.experimental.pallas import tpu as pltpu

F32 = jnp.float32
BF16 = jnp.bfloat16

NORM_EPS = 1e-6
N_SUB = 3
N_MOD = 3
GRID_W = 64
M_HEADS = 8
M_GATES = 4
M_SCAN_CHUNK = 256
M_HEADS_PER_STEP = 4
S5_SB_GROUPS = 16
S5_ROWS = 8
S5_SEG = 256
S5_DOT_SLICE = 256

V7X_VMEM_LIMIT_BYTES = 60000 * 1024
LANES = 128
SUBLANES = 8
ROW_TILES = (1024, 512, 256, 128, 64, 32, 16, 8)


def _pick(n, prefs):
    for p in prefs:
        if p <= n and n % p == 0:
            return p
    return n


def _cparams(n_axes, est_bytes):
    limit = int(min(V7X_VMEM_LIMIT_BYTES, max(est_bytes, 16 * 1024 * 1024)))
    return pltpu.CompilerParams(dimension_semantics=("arbitrary",) * n_axes,
                                vmem_limit_bytes=limit)


def _nbytes(shape, dtype):
    return math.prod(shape) * jnp.dtype(dtype).itemsize


def _ada_body(n_cond, sb_ref, w_ref, b_ref, o_ref):
    k_dim, tn = w_ref.shape
    nl = tn // LANES

    def step(kc, acc):
        k0 = pl.multiple_of(kc * SUBLANES, SUBLANES)
        s = [sb_ref[c, pl.ds(k0, SUBLANES), :] for c in range(n_cond)]
        ws = [w_ref[pl.ds(k0, SUBLANES), l * LANES:(l + 1) * LANES] for l in range(nl)]
        return tuple(acc[c * nl + l] + ws[l] * s[c] for c in range(n_cond) for l in range(nl))

    acc0 = tuple(jnp.zeros((SUBLANES, LANES), F32) for _ in range(n_cond * nl))
    acc = lax.fori_loop(0, k_dim // SUBLANES, step, acc0, unroll=4)
    o_ref[...] = jnp.zeros(o_ref.shape, F32)
    for c in range(n_cond):
        for l in range(nl):
            o_ref[c:c + 1, l * LANES:(l + 1) * LANES] = (
                jnp.sum(acc[c * nl + l], axis=0, keepdims=True) + b_ref[:, l * LANES:(l + 1) * LANES])


def _ada(cond, w_ada, b_ada):
    n_cond, d = cond.shape
    depth, _, n_out = w_ada.shape
    sc = cond * jax.nn.sigmoid(cond)
    sb = jnp.broadcast_to(sc[:, :, None], (n_cond, d, LANES))
    tn = _pick(n_out, (1024, 512, 256, 128))
    est = 2 * (_nbytes((d, tn), F32) + _nbytes((n_cond, d, LANES), F32)) + (4 << 20)
    out = pl.pallas_call(
        functools.partial(_ada_body, n_cond),
        grid=(depth, n_out // tn),
        in_specs=[pl.BlockSpec((n_cond, d, LANES), lambda l, j: (0, 0, 0)),
                  pl.BlockSpec((None, d, tn), lambda l, j: (l, 0, j)),
                  pl.BlockSpec((None, 1, tn), lambda l, j: (l, 0, j))],
        out_specs=pl.BlockSpec((None, SUBLANES, tn), lambda l, j: (l, 0, j)),
        out_shape=jax.ShapeDtypeStruct((depth, SUBLANES, n_out), F32),
        compiler_params=_cparams(2, est),
        name="ada_mod",
    )(sb, w_ada, b_ada.reshape(depth, 1, n_out))
    return out[:, :n_cond]


def _norm_rows(x, g):
    ms = jnp.mean(x * x, axis=-1, keepdims=True)
    return x * lax.rsqrt(ms + NORM_EPS) * g


def _norm_mod_body(x_ref, g_ref, sh_ref, sc_ref, o_ref):
    y = _norm_rows(x_ref[...], g_ref[...])
    o_ref[...] = (y * (1.0 + sc_ref[...]) + sh_ref[...]).astype(o_ref.dtype)


def _norm_body(x_ref, g_ref, o_ref):
    o_ref[...] = _norm_rows(x_ref[...], g_ref[...]).astype(o_ref.dtype)


class _Rows:
    def __init__(self, n_ctx, n_lat_seq, lat_len):
        self.n_ctx = n_ctx
        self.lat_len = lat_len
        self.n = n_ctx + n_lat_seq * lat_len

    def tile(self, prefs=ROW_TILES):
        return _pick(math.gcd(self.n_ctx, self.lat_len), prefs)

    def cond_of_tile(self, i, tm):
        assert self.n_ctx % tm == 0 and self.lat_len % tm == 0
        pt = self.n_ctx // tm
        st = self.lat_len // tm
        return jnp.where(i < pt, 0, 1 + (i - pt) // st)


def _mod_spec(rows, tm, l, slot, tn):
    def imap(i, *rest):
        return (l, rows.cond_of_tile(i, tm), slot, 0, rest[0] if rest else 0)
    return pl.BlockSpec((None, None, None, 1, tn), imap)


def _norm_mod(x, g, mod, rows, l, sub):
    n, d = x.shape
    tm = rows.tile((256, 128, 64, 32, 16, 8))
    est = 2 * (_nbytes((tm, d), F32) + _nbytes((tm, d), BF16)) + 4 * _nbytes((tm, d), F32)
    return pl.pallas_call(
        _norm_mod_body,
        grid=(n // tm,),
        in_specs=[pl.BlockSpec((tm, d), lambda i: (i, 0)),
                  pl.BlockSpec((1, d), lambda i: (0, 0)),
                  _mod_spec(rows, tm, l, sub * N_MOD + 0, d),
                  _mod_spec(rows, tm, l, sub * N_MOD + 1, d)],
        out_specs=pl.BlockSpec((tm, d), lambda i: (i, 0)),
        out_shape=jax.ShapeDtypeStruct((n, d), BF16),
        compiler_params=_cparams(1, est),
        name="norm_mod",
    )(x, g.reshape(1, d), mod, mod)


def _norm_mod_tm(x, g, mod, l, sub, row_off, n_seq, seq_len, n_pad, cond_of_seq):
    n, d = x.shape
    tt = _pick(seq_len, (256, 128, 64, 32, 16, 8))
    nt = seq_len // tt

    def body(x_ref, g_ref, sh_ref, sc_ref, o_ref):
        b = pl.program_id(0)

        @pl.when(b < n_seq)
        def _():
            _norm_mod_body(x_ref, g_ref, sh_ref, sc_ref, o_ref)

        @pl.when(b >= n_seq)
        def _():
            o_ref[...] = jnp.zeros(o_ref.shape, o_ref.dtype)

    def x_map(b, t):
        return (row_off // tt + jnp.minimum(b, n_seq - 1) * nt + t, 0)

    def mod_spec(slot):
        return pl.BlockSpec((None, None, None, 1, d),
                            lambda b, t: (l, cond_of_seq(jnp.minimum(b, n_seq - 1)), slot, 0, 0))

    est = 2 * (_nbytes((tt, d), F32) + _nbytes((tt, d), BF16)) + 4 * _nbytes((tt, d), F32)
    return pl.pallas_call(
        body,
        grid=(n_pad, nt),
        in_specs=[pl.BlockSpec((tt, d), x_map),
                  pl.BlockSpec((1, d), lambda b, t: (0, 0)),
                  mod_spec(sub * N_MOD + 0),
                  mod_spec(sub * N_MOD + 1)],
        out_specs=pl.BlockSpec((tt, d), lambda b, t: (t, b)),
        out_shape=jax.ShapeDtypeStruct((seq_len, n_pad * d), BF16),
        compiler_params=_cparams(2, est),
        name="norm_mod_tm",
    )(x, g.reshape(1, d), mod, mod)


def _final_norm(x, g, row_off, n_rows):
    n, d = x.shape
    tm = _pick(math.gcd(row_off, n_rows) if row_off else n_rows, (256, 128, 64, 32, 16, 8))
    est = 8 * _nbytes((tm, d), F32)
    return pl.pallas_call(
        _norm_body,
        grid=(n_rows // tm,),
        in_specs=[pl.BlockSpec((tm, d), lambda i: (row_off // tm + i, 0)),
                  pl.BlockSpec((1, d), lambda i: (0, 0))],
        out_specs=pl.BlockSpec((tm, d), lambda i: (i, 0)),
        out_shape=jax.ShapeDtypeStruct((n_rows, d), F32),
        compiler_params=_cparams(1, est),
        name="final_norm",
    )(x, g.reshape(1, d))


def _dot(a, b):
    return jnp.dot(a, b, preferred_element_type=F32)


def _mm_body(kind, coef, n_w, has_side, w_t, *refs):
    a_ref, w_refs, rest = refs[0], refs[1:1 + n_w], refs[1 + n_w:]
    if has_side:
        side_in, rest, side_out = rest[0], rest[1:-1], rest[-1]
        side_out[...] = side_in[...].astype(BF16)
    ws = [w[...].astype(BF16) for w in w_refs]
    if w_t:
        acc = lax.dot_general(a_ref[...], ws[0], (((1,), (1,)), ((), ())), preferred_element_type=F32)
        tn = w_refs[0].shape[0]
    else:
        acc = _dot(a_ref[...], ws[0] if n_w == 1 else jnp.concatenate(ws, axis=1))
        tn = w_refs[0].shape[1]
    if kind == "plain":
        (o_ref,) = rest
        o_ref[...] = acc.astype(o_ref.dtype)
    elif kind == "swiglu":
        (o_ref,) = rest
        g, u = acc[:, :tn], acc[:, tn:]
        o_ref[...] = (g * jax.nn.sigmoid(g) * u).astype(o_ref.dtype)
    elif kind == "resid":
        res_ref, gate_ref, o_ref = rest
        o_ref[...] = res_ref[...] + (coef * gate_ref[...]) * acc
    elif kind == "glu_resid":
        res_ref, gate_ref, o_ref = rest
        o_ref[...] = res_ref[...] + gate_ref[...] * (acc[:, :tn] * jax.nn.sigmoid(acc[:, tn:]))
    else:
        raise ValueError(kind)


def _mm(kind, a, w, lead, col_blocks, tn, n_out, out_dtype, *, k_part=(0, 1), coef=1.0,
        res=None, gate=None, tm=None, side=None, w_t=False):
    m = a.shape[0]
    kp = w.shape[-1 if w_t else -2] // k_part[1]
    n_w = len(col_blocks)
    assert not (w_t and n_w > 1)
    if tm is None:
        tm = _pick(m, ROW_TILES)
    n_i, n_j = m // tm, n_out // tn
    n_lead = len(lead)
    in_specs = [pl.BlockSpec((tm, kp), lambda i, j: (i, k_part[0]))]
    if w_t:
        in_specs += [pl.BlockSpec((None,) * n_lead + (tn, kp),
                                  lambda i, j, cb=cb: tuple(lead) + (cb(j), k_part[0]))
                     for cb in col_blocks]
    else:
        in_specs += [pl.BlockSpec((None,) * n_lead + (kp, tn),
                                  lambda i, j, cb=cb: tuple(lead) + (k_part[0], cb(j)))
                     for cb in col_blocks]
    args = [a] + [w] * n_w
    out_specs = [pl.BlockSpec((tm, tn), lambda i, j: (i, j))]
    out_shape = [jax.ShapeDtypeStruct((m, n_out), out_dtype)]
    side_bytes = 0
    if side is not None:
        s_arr, s_lead = side
        s_rows, s_cols = s_arr.shape[-2:]
        assert s_rows % n_j == 0 and s_cols % n_i == 0
        sblk = (s_rows // n_j, s_cols // n_i)
        in_specs.append(pl.BlockSpec((None,) * len(s_lead) + sblk, lambda i, j: tuple(s_lead) + (j, i)))
        args.append(s_arr)
        out_specs.append(pl.BlockSpec(sblk, lambda i, j: (j, i)))
        out_shape.append(jax.ShapeDtypeStruct((s_rows, s_cols), BF16))
        side_bytes = 2 * (_nbytes(sblk, F32) + _nbytes(sblk, BF16))
    if res is not None:
        mod, rows, l, slot = gate
        in_specs += [pl.BlockSpec((tm, tn), lambda i, j: (i, j)), _mod_spec(rows, tm, l, slot, tn)]
        args += [res, mod]
    est = (2 * (_nbytes((tm, kp), BF16) + n_w * _nbytes((kp, tn), w.dtype) + _nbytes((tm, tn), out_dtype))
           + n_w * _nbytes((kp, tn), BF16) + 3 * n_w * _nbytes((tm, tn), F32)
           + (4 * _nbytes((tm, tn), F32) if res is not None else 0) + side_bytes + (2 << 20))
    outs = pl.pallas_call(
        functools.partial(_mm_body, kind, coef, n_w, side is not None, w_t),
        grid=(n_i, n_j),
        in_specs=in_specs,
        out_specs=out_specs,
        out_shape=out_shape,
        compiler_params=_cparams(2, est),
        name="mm_" + kind,
    )(*args)
    return outs if side is not None else outs[0]


def _side_cast_row_tile(m, n_cols):
    for tm in (1280, 1024, 640, 512, 256, 128, 64, 32, 16):
        if m % tm == 0 and n_cols % (m // tm) == 0 and (n_cols // (m // tm)) % LANES == 0:
            return tm
    return None


def _conv_body(grid_rows, n_q_blocks, q_scale, x_ref, w_ref, b_ref, o_ref):
    x = x_ref[...]
    t_len = x.shape[0]
    t_idx = lax.broadcasted_iota(jnp.int32, x.shape, 0)
    if grid_rows is None:
        taps = [(0, dc) for dc in (-1, 0, 1)]
        col = t_idx
        n_cols = t_len
        row = jnp.zeros_like(t_idx)
        n_rows = 1
    else:
        taps = [(dr, dc) for dr in (-1, 0, 1) for dc in (-1, 0, 1)]
        col = t_idx % GRID_W
        n_cols = GRID_W
        row = t_idx // GRID_W
        n_rows = grid_rows
    acc = jnp.zeros(x.shape, F32)
    for dr, dc in taps:
        off = dr * n_cols + dc
        xs = x if off == 0 else pltpu.roll(x, shift=(-off) % t_len, axis=0)
        ok = ((col + dc >= 0) & (col + dc < n_cols) & (row + dr >= 0) & (row + dr < n_rows))
        wv = w_ref[dr + 1, dc + 1:dc + 2, :]
        acc = acc + jnp.where(ok, xs, 0.0) * wv
    y = acc + b_ref[...]
    y = y * jax.nn.sigmoid(y)
    scale = jnp.where(pl.program_id(1) < n_q_blocks, q_scale, 1.0)
    o_ref[...] = (y * scale).astype(o_ref.dtype)


def _qk_conv(proj, conv_w, conv_b, row_off, n_seq, seq_len, grid_rows, dk):
    ch = conv_w.shape[-1]
    tc = _pick(ch // 2, (512, 256, 128))
    n_q_blocks = (ch // 2) // tc
    body = functools.partial(_conv_body, grid_rows, n_q_blocks, float(dk) ** -0.5)
    est = 2 * (_nbytes((seq_len, tc), F32) + _nbytes((seq_len, tc), BF16)) + 8 * _nbytes((seq_len, tc), F32)
    return pl.pallas_call(
        body,
        grid=(n_seq, ch // tc),
        in_specs=[pl.BlockSpec((seq_len, tc), lambda b, c: (row_off // seq_len + b, c)),
                  pl.BlockSpec((3, 3, tc), lambda b, c: (0, 0, c)),
                  pl.BlockSpec((1, tc), lambda b, c: (0, c))],
        out_specs=pl.BlockSpec((seq_len, tc), lambda b, c: (b, c)),
        out_shape=jax.ShapeDtypeStruct((n_seq * seq_len, ch), BF16),
        compiler_params=_cparams(2, est),
        name="qk_conv",
    )(proj, conv_w, conv_b.reshape(1, ch))


def _log_sigmoid(x):
    return jnp.minimum(x, 0.0) - jnp.log1p(jnp.exp(-jnp.abs(x)))


def _mlstm_decay(backward, li_col, fp_col, li_row, fp_row, m_prev):
    l = li_col.shape[0]
    lf_col = _log_sigmoid(fp_col)
    lf_row = _log_sigmoid(fp_row)
    t_i = lax.broadcasted_iota(jnp.int32, (l, l), 0)
    s_i = lax.broadcasted_iota(jnp.int32, (l, l), 1)
    causal = (s_i >= t_i) if backward else (s_i <= t_i)
    causal_t = (t_i >= s_i) if backward else (t_i <= s_i)
    b_col = jnp.sum(jnp.where(causal, lf_row, 0.0), axis=1, keepdims=True)
    b_row = jnp.sum(jnp.where(causal_t, lf_col, 0.0), axis=0, keepdims=True)
    b_last = jnp.sum(lf_col, axis=0, keepdims=True)
    log_d = jnp.where(causal, b_col - b_row + li_row, -jnp.inf)
    log_inter = b_col + m_prev
    m_out = jnp.maximum(log_inter, jnp.max(log_d, axis=1, keepdims=True))
    dmat = jnp.exp(log_d - m_out)
    s_inter = jnp.exp(log_inter - m_out)
    log_w = b_last - b_col + li_col
    m_new = jnp.maximum(b_last + m_prev, jnp.max(log_w, axis=0, keepdims=True))
    w = jnp.exp(log_w - m_new)
    decay = jnp.exp(b_last + m_prev - m_new)
    return dmat, s_inter, jnp.exp(-m_out), w, decay, m_new


def _mlstm_chunk(chains):
    qk = [lax.dot_general(ch["q"], ch["k"], (((1,), (1,)), ((), ())), preferred_element_type=F32)
          for ch in chains]
    qc = [None if ch["c"] is None else _dot(ch["q"], ch["c"].astype(BF16)) for ch in chains]
    dec = [_mlstm_decay(ch["backward"], ch["li_col"], ch["fp_col"], ch["li_row"], ch["fp_row"], ch["m"])
           for ch in chains]
    s, den, kw, n_new = [], [], [], []
    for ch, (dmat, s_inter, _, w, decay, _), qk_i in zip(chains, dec, qk):
        s_i = qk_i * dmat
        s.append(s_i)
        den_i = jnp.sum(s_i, axis=1, keepdims=True)
        kw_i = ch["k"].astype(F32) * w
        n_i = jnp.sum(kw_i, axis=0, keepdims=True)
        if ch["n"] is not None:
            den_i = den_i + s_inter * jnp.sum(ch["q"].astype(F32) * ch["n"], axis=1, keepdims=True)
            n_i = decay * ch["n"] + n_i
        den.append(den_i)
        kw.append(kw_i)
        n_new.append(n_i)
    sv = [_dot(s_i.astype(BF16), ch["v"]) for ch, s_i in zip(chains, s)]
    kv = [lax.dot_general(kw_i.astype(BF16), ch["v"], (((0,), (0,)), ((), ())), preferred_element_type=F32)
          for ch, kw_i in zip(chains, kw)]
    out = []
    for ch, (_, s_inter, exp_neg_m, _, decay, m_new), qc_i, den_i, sv_i, kv_i, n_i in zip(
            chains, dec, qc, den, sv, kv, n_new):
        num = sv_i if qc_i is None else sv_i + s_inter * qc_i
        h = num / jnp.maximum(jnp.abs(den_i), exp_neg_m)
        out.append((h, kv_i if ch["c"] is None else decay * ch["c"] + kv_i, n_i, m_new))
    return out


def _mlstm_body(zero_init, emit_state, single_chunk, n_heads, dk, dv, *refs):
    it = iter(refs)
    fwd_in = tuple(next(it) for _ in range(5))
    bwd_in = tuple(next(it) for _ in range(5))
    if not zero_init:
        c0, n0, m0 = next(it), next(it), next(it)
    if emit_state:
        next(it), next(it), next(it)
    hf_o, hb_o = next(it), next(it)
    if emit_state:
        c_o, n_o, m_o = next(it), next(it), next(it)
    carried = not (zero_init and single_chunk)
    if carried:
        c_s, n_s, m_s = next(it), next(it), next(it)
        c_idx = pl.program_id(2)

        @pl.when(c_idx == 0)
        def _():
            if zero_init:
                c_s[...] = jnp.zeros(c_s.shape, F32)
                n_s[...] = jnp.zeros(n_s.shape, F32)
                m_s[...] = jnp.zeros(m_s.shape, F32)
            else:
                c_s[...] = c0[...]
                n_s[...] = n0[...]
                m_s[...] = m0[...]

        n_all, m_all = n_s[...], m_s[...]
    else:
        m_all = jnp.zeros((2, n_heads), F32)
    head_row = lax.broadcasted_iota(jnp.int32, (n_heads, dk), 0)
    m_dir = lax.broadcasted_iota(jnp.int32, m_all.shape, 0)
    m_head = lax.broadcasted_iota(jnp.int32, m_all.shape, 1)
    chains = []
    for d, (q_ref, k_ref, v_ref, gc_ref, gr_ref) in enumerate((fwd_in, bwd_in)):
        for h in range(n_heads):
            gi, gf = (2 * d) * n_heads + h, (2 * d + 1) * n_heads + h
            chains.append(dict(
                backward=(d == 1),
                q=q_ref[:, h * dk:(h + 1) * dk], k=k_ref[:, h * dk:(h + 1) * dk],
                v=v_ref[:, h * dv:(h + 1) * dv],
                li_col=gc_ref[:, gi:gi + 1], fp_col=gc_ref[:, gf:gf + 1],
                li_row=gr_ref[gi:gi + 1, :], fp_row=gr_ref[gf:gf + 1, :],
                c=c_s[d, h] if carried else None, n=n_all[d, h:h + 1, :] if carried else None,
                m=m_all[d:d + 1, h:h + 1]))
    results = _mlstm_chunk(chains)
    m_next = m_all
    n_next = []
    for d, h_o in enumerate((hf_o, hb_o)):
        n_d = jnp.zeros((n_heads, dk), F32)
        for h in range(n_heads):
            hh, c_new, n_new, m_new = results[d * n_heads + h]
            h_o[:, h * dv:(h + 1) * dv] = hh
            if carried:
                c_s[d, h] = c_new
            else:
                c_o[d, h] = c_new
            n_d = jnp.where(head_row == h, n_new, n_d)
            m_next = jnp.where((m_dir == d) & (m_head == h), m_new, m_next)
        n_next.append(n_d)
    if carried:
        for d in range(2):
            n_s[d] = n_next[d]
        m_s[...] = m_next
        if emit_state:
            @pl.when(c_idx == pl.num_programs(2) - 1)
            def _():
                c_o[...] = c_s[...]
                n_o[...] = n_s[...]
                m_o[...] = m_s[...]
    else:
        for d in range(2):
            n_o[d] = n_next[d]
        m_o[...] = m_next


def _mlstm_scan(qk, v, gates, row_off, n_seq, seq_len, dk, dv, layer_j, state_in=None, state_out=None):
    nh = M_HEADS
    hb = _pick(nh, (M_HEADS_PER_STEP, 2, 1))
    n_hg = nh // hb
    lc = _pick(seq_len, (M_SCAN_CHUNK, 128, 64, 32, 16, 8))
    nc = seq_len // lc
    n_rows = n_seq * seq_len
    zero_init = state_in is None
    emit_state = state_out is not None
    carried = not (zero_init and nc == 1)
    assert carried or emit_state
    n_tot = gates.shape[0]
    g5 = gates.reshape(n_tot // lc, lc, M_GATES, n_hg, hb)
    g_col = jnp.transpose(g5, (3, 0, 1, 2, 4)).reshape(n_hg, n_tot // lc, lc, M_GATES * hb)
    g_row = jnp.swapaxes(g_col, 2, 3)
    off_c = row_off // lc

    def specs(cmap):
        return [pl.BlockSpec((lc, hb * dk), lambda b, g, c: (cmap(b, c), g)),
                pl.BlockSpec((lc, hb * dk), lambda b, g, c: (cmap(b, c), n_hg + g)),
                pl.BlockSpec((lc, hb * dv), lambda b, g, c: (off_c + cmap(b, c), g)),
                pl.BlockSpec((None, None, lc, M_GATES * hb), lambda b, g, c: (g, off_c + cmap(b, c), 0, 0)),
                pl.BlockSpec((None, None, M_GATES * hb, lc), lambda b, g, c: (g, off_c + cmap(b, c), 0, 0))]

    def fwd(b, c):
        return b * nc + c

    def bwd(b, c):
        return b * nc + (nc - 1 - c)

    def group_views(state):
        c_a, n_a, m_a = state
        lead = c_a.shape[:3]
        return (c_a.reshape(lead + (n_hg, hb, dk, dv)), n_a.reshape(lead + (n_hg, hb, dk)),
                jnp.swapaxes(m_a.reshape(lead + (n_hg, hb)), 2, 3))

    def state_specs():
        return [pl.BlockSpec((None, None, 2, None, hb, dk, dv), lambda b, g, c: (b, layer_j, 0, g, 0, 0, 0)),
                pl.BlockSpec((None, None, 2, None, hb, dk), lambda b, g, c: (b, layer_j, 0, g, 0, 0)),
                pl.BlockSpec((None, None, None, 2, hb), lambda b, g, c: (b, layer_j, g, 0, 0))]

    in_specs = specs(fwd) + specs(bwd)
    args = [qk, qk, v, g_col, g_row] * 2
    if not zero_init:
        in_specs += state_specs()
        args += list(group_views(state_in))
    aliases = {}
    if emit_state:
        out_views = group_views(state_out)
        for t, s in enumerate(out_views):
            aliases[len(args)] = 2 + t
            in_specs.append(pl.BlockSpec(memory_space=pl.ANY))
            args.append(s)
    out_specs = [pl.BlockSpec((lc, hb * dv), lambda b, g, c: (fwd(b, c), g)),
                 pl.BlockSpec((lc, hb * dv), lambda b, g, c: (bwd(b, c), g))]
    out_shape = [jax.ShapeDtypeStruct((n_rows, nh * dv), F32)] * 2
    if emit_state:
        out_specs += state_specs()
        out_shape += [jax.ShapeDtypeStruct(s.shape, F32) for s in out_views]
    scratch = []
    if carried:
        scratch = [pltpu.VMEM((2, hb, dk, dv), F32), pltpu.VMEM((2, hb, dk), F32), pltpu.VMEM((2, hb), F32)]
    state_bytes = _nbytes((2, hb, dk, dv), F32)
    chain_bytes = 4 * _nbytes((lc, lc), F32) + 4 * _nbytes((lc, dv), F32) + 2 * _nbytes((dk, dv), F32)
    est = (state_bytes * ((1 if carried else 0) + (2 if not zero_init else 0) + (2 if emit_state else 0))
           + 2 * hb * chain_bytes + 8 * _nbytes((lc, hb * dv), F32) + 16 * _nbytes((lc, hb * dk), BF16)
           + (4 << 20))
    outs = pl.pallas_call(
        functools.partial(_mlstm_body, zero_init, emit_state, nc == 1, hb, dk, dv),
        grid=(n_seq, n_hg, nc),
        in_specs=in_specs,
        out_specs=out_specs,
        out_shape=out_shape,
        scratch_shapes=scratch,
        input_output_aliases=aliases,
        compiler_params=_cparams(3, est),
        name="mlstm_scan",
    )(*args)
    new_state = None
    if emit_state:
        c_a, n_a, m_a = state_out
        new_state = (outs[2].reshape(c_a.shape), outs[3].reshape(n_a.shape),
                     jnp.swapaxes(outs[4], 2, 3).reshape(m_a.shape))
    return outs[0], outs[1], new_state


def _mlstm_combine_body(hf_ref, hb_ref, o_ref, g_ref, z_ref):
    hh = _norm_rows(hf_ref[...] + hb_ref[...], g_ref[...])
    z_ref[...] = (hh * jax.nn.sigmoid(o_ref[...])).astype(z_ref.dtype)


def _mlstm_combine(h_f, h_b, o_gate, head_g, dv, row_off):
    n_rows, width = h_f.shape
    tm = _pick(math.gcd(n_rows, row_off) if row_off else n_rows, (512, 256, 128, 64, 32, 16, 8))
    est = 2 * (3 * _nbytes((tm, dv), F32) + _nbytes((tm, dv), BF16)) + 4 * _nbytes((tm, dv), F32)
    return pl.pallas_call(
        _mlstm_combine_body,
        grid=(n_rows // tm, M_HEADS),
        in_specs=[pl.BlockSpec((tm, dv), lambda i, h: (i, h)),
                  pl.BlockSpec((tm, dv), lambda i, h: (i, h)),
                  pl.BlockSpec((tm, dv), lambda i, h: (row_off // tm + i, h)),
                  pl.BlockSpec((1, dv), lambda i, h: (0, h))],
        out_specs=pl.BlockSpec((tm, dv), lambda i, h: (i, h)),
        out_shape=jax.ShapeDtypeStruct((n_rows, width), BF16),
        compiler_params=_cparams(2, est),
        name="mlstm_combine",
    )(h_f, h_b, o_gate, head_g.reshape(1, width))


def _s5_body(zero_init, emit_y, emit_state, seq_len, tc_len, *refs):
    it = iter(refs)
    u_ref, bw_ref, ar_ref, ai_ref = (next(it) for _ in range(4))
    cw_ref = next(it) if emit_y else None
    if not zero_init:
        x0r_ref, x0i_ref = next(it), next(it)
    y_ref = next(it) if emit_y else None
    if emit_state:
        fr_ref, fi_ref = next(it), next(it)
    e_bufs = (next(it), next(it))
    xb_s = next(it) if emit_y else None
    ns = ar_ref.shape[-1]
    cin = u_ref.shape[-1]
    n_tc = seq_len // tc_len
    rows_c = tc_len * S5_ROWS
    pair_rows = 2 * S5_ROWS
    n_pairs = tc_len // 2
    slice_w = S5_DOT_SLICE if (2 * ns) % S5_DOT_SLICE == 0 else 2 * ns
    n_sl = (2 * ns) // slice_w
    stages = [(d, c) for d in range(2) for c in (range(n_tc) if d == 0 else range(n_tc - 1, -1, -1))]

    def u_chunk(c):
        return u_ref[c * tc_len:(c + 1) * tc_len].reshape(rows_c, cin).astype(BF16)

    def e_slice(buf, d, u2, sl):
        buf[:, sl * slice_w:(sl + 1) * slice_w] = _dot(u2, bw_ref[d, :, sl * slice_w:(sl + 1) * slice_w])

    u2 = u_chunk(stages[0][1])
    for sl in range(n_sl):
        e_slice(e_bufs[0], stages[0][0], u2, sl)

    for k, (d, c) in enumerate(stages):
        cur, nxt = e_bufs[k % 2], e_bufs[(k + 1) % 2]
        first_c, last_c = (0, n_tc - 1) if d == 0 else (n_tc - 1, 0)
        if c == first_c:
            ar = jnp.broadcast_to(ar_ref[d], (S5_ROWS, ns))
            ai = jnp.broadcast_to(ai_ref[d], (S5_ROWS, ns))
            if zero_init:
                xr = xi = jnp.zeros((S5_ROWS, ns), F32)
            else:
                xr, xi = x0r_ref[d], x0i_ref[d]
        nxt_stage = stages[k + 1] if k + 1 < len(stages) else None
        if nxt_stage is not None:
            u2n = u_chunk(nxt_stage[1])
        sl_done = 0

        def step(r, xr, xi, cur=cur, ar=ar, ai=ai):
            er = cur[r:r + S5_ROWS, 0:ns]
            ei = cur[r:r + S5_ROWS, ns:2 * ns]
            return ar * xr - ai * xi + er, ar * xi + ai * xr + ei

        for p in range(n_pairs):
            while nxt_stage is not None and sl_done < n_sl and sl_done * n_pairs // n_sl <= p:
                e_slice(nxt, nxt_stage[0], u2n, sl_done)
                sl_done += 1
            q = p if d == 0 else n_pairs - 1 - p
            r_lo = q * pair_rows
            r_hi = r_lo + S5_ROWS
            x1 = step(r_lo if d == 0 else r_hi, xr, xi)
            xr, xi = step(r_hi if d == 0 else r_lo, *x1)
            if emit_y:
                lo, hi = (x1, (xr, xi)) if d == 0 else ((xr, xi), x1)
                r_out = c * rows_c + r_lo
                for part in range(2):
                    col = (2 * d + part) * ns
                    xb_s[r_out:r_out + pair_rows, col:col + ns] = (
                        jnp.concatenate([lo[part], hi[part]], axis=0).astype(BF16))
        while nxt_stage is not None and sl_done < n_sl:
            e_slice(nxt, nxt_stage[0], u2n, sl_done)
            sl_done += 1
        if emit_state and c == last_c:
            fr_ref[d] = xr
            fi_ref[d] = xi

    if emit_y:
        y = _dot(xb_s[...], cw_ref[...])
        y_ref[...] = y.reshape(y_ref.shape)


def _s5_scan(u_tm, bw, a_re, a_im, cw, state, emit_y, emit_state):
    seq_len, n_pad, e_width = u_tm.shape
    n_sb = bw.shape[1]
    cin = e_width // n_sb
    ns = bw.shape[-1] // 2
    tc_len = _pick(seq_len, (64, 32, 16, 8, 4, 2))
    n_bg = n_pad // S5_ROWS
    zero_init = state is None

    def wspec(r, c):
        return pl.BlockSpec((2, None, r, c), lambda bg, sb: (0, sb, 0, 0))

    def sspec():
        return pl.BlockSpec((2, S5_ROWS, ns), lambda bg, sb: (0, bg, sb))

    in_specs = [pl.BlockSpec((seq_len, S5_ROWS, cin), lambda bg, sb: (0, bg, sb)),
                wspec(cin, 2 * ns), wspec(1, ns), wspec(1, ns)]
    args = [u_tm, bw, a_re, a_im]
    if emit_y:
        in_specs.append(pl.BlockSpec((None, 4 * ns, cin), lambda bg, sb: (sb, 0, 0)))
        args.append(cw)
    if not zero_init:
        in_specs += [sspec(), sspec()]
        args += list(state)
    out_specs, out_shape = [], []
    scratch = [pltpu.VMEM((tc_len * S5_ROWS, 2 * ns), F32)] * 2
    if emit_y:
        out_specs.append(pl.BlockSpec((seq_len, S5_ROWS, cin), lambda bg, sb: (0, bg, sb)))
        out_shape.append(jax.ShapeDtypeStruct((seq_len, n_pad, e_width), F32))
        scratch.append(pltpu.VMEM((seq_len * S5_ROWS, 4 * ns), BF16))
    if emit_state:
        out_specs += [sspec(), sspec()]
        out_shape += [jax.ShapeDtypeStruct((2, n_pad, n_sb * ns), F32)] * 2
    est = (3 * _nbytes((tc_len * S5_ROWS, 2 * ns), F32) + _nbytes((seq_len * S5_ROWS, 4 * ns), BF16)
           + 5 * _nbytes((seq_len, S5_ROWS, cin), F32) + 8 * _nbytes((cin, 2 * ns), BF16) + (4 << 20))
    outs = pl.pallas_call(
        functools.partial(_s5_body, zero_init, emit_y, emit_state, seq_len, tc_len),
        grid=(n_bg, n_sb),
        in_specs=in_specs,
        out_specs=out_specs,
        out_shape=out_shape,
        scratch_shapes=scratch,
        compiler_params=_cparams(2, est),
        name="s5_scan",
    )(*args)
    y = outs[0] if emit_y else None
    fin = tuple(outs[-2:]) if emit_state else None
    return y, fin


def _s5_carry_body(n_seq, n_seg, fr_ref, fi_ref, x0r_ref, x0i_ref, pr_ref, pi_ref, or_ref, oi_ref):
    or_ref[...] = jnp.zeros(or_ref.shape, F32)
    oi_ref[...] = jnp.zeros(oi_ref.shape, F32)
    for d in range(2):
        pr, pi = pr_ref[d], pi_ref[d]
        order = range(n_seg) if d == 0 else range(n_seg - 1, -1, -1)
        for b in range(n_seq):
            cr, ci = x0r_ref[d, b:b + 1, :], x0i_ref[d, b:b + 1, :]
            for k in order:
                r = b * n_seg + k
                or_ref[d, r:r + 1, :] = cr
                oi_ref[d, r:r + 1, :] = ci
                fr, fi = fr_ref[d, r:r + 1, :], fi_ref[d, r:r + 1, :]
                cr, ci = pr * cr - pi * ci + fr, pr * ci + pi * cr + fi


def _s5_carry(fin, x0, a_pow, n_seq, n_seg):
    shape = fin[0].shape
    return pl.pallas_call(
        functools.partial(_s5_carry_body, n_seq, n_seg),
        out_shape=[jax.ShapeDtypeStruct(shape, F32)] * 2,
        compiler_params=_cparams(0, 16 * _nbytes(shape, F32)),
        name="s5_carry",
    )(fin[0], fin[1], x0[0], x0[1], a_pow[0], a_pow[1])


def _s5_gelu_body(y_ref, u_ref, d_ref, z_ref, v_s):
    c = math.sqrt(2.0 / math.pi)
    v_s[...] = y_ref[...] + d_ref[...] * u_ref[...]
    for b in range(S5_ROWS):
        v = v_s[:, b, :]
        z = 0.5 * v * (1.0 + jnp.tanh(c * (v + 0.044715 * (v * v * v))))
        z_ref[b] = z.astype(z_ref.dtype)


def _s5_gelu(y, u_tm, d_skip):
    seq_len, n_pad, e_width = u_tm.shape
    tt = _pick(seq_len, (128, 64, 32, 16, 8))
    ec = _pick(e_width, (512, 256, 128))
    blk = (tt, S5_ROWS, ec)
    est = 2 * (2 * _nbytes(blk, F32) + _nbytes(blk, BF16)) + 8 * _nbytes(blk, F32)
    return pl.pallas_call(
        _s5_gelu_body,
        grid=(n_pad // S5_ROWS, seq_len // tt, e_width // ec),
        in_specs=[pl.BlockSpec(blk, lambda bg, t, e: (t, bg, e)),
                  pl.BlockSpec(blk, lambda bg, t, e: (t, bg, e)),
                  pl.BlockSpec((1, ec), lambda bg, t, e: (0, e))],
        out_specs=pl.BlockSpec((S5_ROWS, tt, ec), lambda bg, t, e: (bg, t, e)),
        out_shape=jax.ShapeDtypeStruct((n_pad, seq_len, e_width), BF16),
        scratch_shapes=[pltpu.VMEM(blk, F32)],
        compiler_params=_cparams(3, est),
        name="s5_gelu",
    )(y, u_tm, d_skip.reshape(1, e_width))


def _s5_weights(lam_re, lam_im, log_dt, b_re, b_im, c_re, c_im, seg_len):
    lr = jnp.minimum(lam_re.astype(F32), -1e-4)
    li = lam_im.astype(F32)
    dt = jnp.exp(log_dt.astype(F32))[..., None]
    mag = jnp.exp(lr * dt)
    ar = mag * jnp.cos(li * dt)
    ai = mag * jnp.sin(li * dt)
    den = lr * lr + li * li
    xr = ar - 1.0
    cr = (xr * lr + ai * li) / den
    ci = (ai * lr - xr * li) / den
    bp_re = cr[..., None] * b_re - ci[..., None] * b_im
    bp_im = cr[..., None] * b_im + ci[..., None] * b_re
    n_dir, g, p = lam_re.shape
    gc = b_re.shape[-1]
    sbg = min(S5_SB_GROUPS, g)
    n_sb = g // sbg
    same_group = (lax.broadcasted_iota(jnp.int32, (sbg * gc, sbg * p), 0) // gc
                  == lax.broadcasted_iota(jnp.int32, (sbg * gc, sbg * p), 1) // p)

    def bd_in(bp):
        t = jnp.transpose(bp.reshape(n_dir, n_sb, sbg, p, gc), (0, 1, 4, 2, 3))
        t = jnp.tile(t.reshape(n_dir, n_sb, gc, sbg * p), (1, 1, sbg, 1))
        return jnp.where(same_group, t, 0.0)

    def bd_out(cc):
        t = jnp.transpose(cc.reshape(n_dir, n_sb, sbg, gc, p), (0, 1, 2, 4, 3))
        t = jnp.tile(t.reshape(n_dir, n_sb, sbg * p, gc), (1, 1, 1, sbg))
        return jnp.where(same_group.T, t, 0.0)

    bw = jnp.concatenate([bd_in(bp_re), bd_in(bp_im)], axis=-1).astype(BF16)
    cw = jnp.concatenate([bd_out(c_re.astype(F32)), -bd_out(c_im.astype(F32))], axis=-2).astype(BF16)
    cw = jnp.swapaxes(cw, 0, 1).reshape(n_sb, n_dir * 2 * sbg * p, sbg * gc)
    a_re = ar.reshape(n_dir, n_sb, 1, sbg * p)
    a_im = ai.reshape(n_dir, n_sb, 1, sbg * p)
    pr, pi = jnp.ones_like(ar), jnp.zeros_like(ai)
    br, bi = ar, ai
    e = seg_len
    while e:
        if e & 1:
            pr, pi = pr * br - pi * bi, pr * bi + pi * br
        br, bi = br * br - bi * bi, 2.0 * br * bi
        e >>= 1
    a_pow = (pr.reshape(n_dir, 1, g * p), pi.reshape(n_dir, 1, g * p))
    return bw, a_re, a_im, cw, a_pow


def _s5_group(x, g, mod, l, w_in, lead, weights, d_skip, row_off, n_seq, seq_len, cond0, per_seq_cond,
              state, want_state):
    bw, a_re, a_im, cw, a_pow = weights
    d = x.shape[1]
    seg = _pick(seq_len, (S5_SEG, 128, 64, 32, 16, 8))
    n_seg = seq_len // seg
    n_ps = n_seq * n_seg
    n_pad = -(-n_ps // S5_ROWS) * S5_ROWS

    def cond_of(p):
        return cond0 + (p // n_seg if per_seq_cond else 0)

    h_tm = _norm_mod_tm(x, g, mod, l, 1, row_off, n_ps, seg, n_pad, cond_of)
    e_w = w_in.shape[-1]
    u_tm = _mm("plain", h_tm.reshape(seg * n_pad, d), w_in, lead, [lambda j: j],
               _pick(e_w, (512, 256, 128)), e_w, F32).reshape(seg, n_pad, e_w)

    def pad_rows(s):
        return jnp.pad(s, ((0, 0), (0, n_pad - s.shape[1]), (0, 0)))

    if n_seg == 1:
        start = None if state is None else tuple(pad_rows(s) for s in state)
    else:
        assert not want_state
        _, fin = _s5_scan(u_tm, bw, a_re, a_im, cw, None, emit_y=False, emit_state=True)
        x0 = state if state is not None else tuple(jnp.zeros((2, n_seq, fin[0].shape[-1]), F32) for _ in range(2))
        start = _s5_carry(fin, x0, a_pow, n_seq, n_seg)
    y, fin = _s5_scan(u_tm, bw, a_re, a_im, cw, start, emit_y=True, emit_state=want_state)
    z = _s5_gelu(y, u_tm, d_skip).reshape(n_pad * seg, e_w)[:n_seq * seq_len]
    if want_state:
        fin = tuple(f[:, :n_seq] for f in fin)
    return z, fin


def kernel(x_prompt, x_sample, state_mlstm_C, state_mlstm_n, state_mlstm_m, state_s5_re, state_s5_im, c, c_ctx, w_ada, b_ada, norm_g, final_g, w_ffn_in, w_ffn_out, m_w_in, m_conv_w, m_conv_b, m_gate_b, m_head_g, m_w_out, s5_w_in, s5_lam_re, s5_lam_im, s5_log_dt, s5_b_re, s5_b_im, s5_c_re, s5_c_im, s5_d, s5_w_out):
    bp, tp, d = x_prompt.shape
    bs, ts, _ = x_sample.shape
    depth = w_ada.shape[0]
    n_ctx, n_lat = bp * tp, bs * ts
    rows = _Rows(n_ctx, bs, ts)
    tm = rows.tile()
    x = jnp.concatenate([x_prompt.reshape(n_ctx, d), x_sample.reshape(n_lat, d)], axis=0)

    cond = jnp.concatenate([c_ctx[None, :], c], axis=0).astype(F32)
    mod = _ada(cond, w_ada, b_ada).reshape(depth, 1 + bs, N_SUB * N_MOD, 1, d)

    n_ml, n_s5 = m_w_in.shape[0], s5_w_in.shape[0]
    dk, dv = state_mlstm_C.shape[-2], state_mlstm_C.shape[-1]
    qk_w = 2 * M_HEADS * dk
    v_w = M_HEADS * dv
    main_w = qk_w + 2 * v_w
    m_w_t = jnp.swapaxes(m_w_in, 1, 2)
    g_groups, p_state = state_s5_re.shape[-2], state_s5_re.shape[-1]
    d_ff = w_ffn_out.shape[-2]

    def gate_of(l, sub):
        return (mod, rows, l, sub * N_MOD + 2)

    def ffn(x, l, idx, sub):
        h = _norm_mod(x, norm_g[l, sub], mod, rows, l, sub)
        tn = _pick(d_ff, (256, 128))
        nf = d_ff // tn
        up_cols = [lambda j: j, lambda j: nf + j]
        tm_side = _side_cast_row_tile(rows.n, d)
        if tm_side is not None:
            a, w_out = _mm("swiglu", h, w_ffn_in, (l, idx), up_cols, tn, d_ff, BF16, tm=tm_side,
                           side=(w_ffn_out, (l, idx)))
            w_lead, tn_out = (), _pick(d, (512, 256, 128))
        else:
            a = _mm("swiglu", h, w_ffn_in, (l, idx), up_cols, tn, d_ff, BF16)
            w_out, w_lead, tn_out = w_ffn_out, (l, idx), _pick(d, (256, 128))
        k_parts = 2 if d_ff % (2 * LANES) == 0 and d_ff > 4096 else 1
        for p in range(k_parts):
            x = _mm("resid", a, w_out, w_lead, [lambda j: j], tn_out, d, F32,
                    k_part=(p, k_parts), coef=0.5, res=x, gate=gate_of(l, sub), tm=tm)
        return x

    def mlstm(x, l, j, new_state):
        h = _norm_mod(x, norm_g[l, 1], mod, rows, l, 1)
        tn = _pick(math.gcd(qk_w, v_w), (512, 256, 128))
        qk_pre = _mm("plain", h, m_w_t, (j,), [lambda jb: jb], tn, qk_w, F32, w_t=True)
        v = _mm("plain", h, m_w_t, (j,), [lambda jb: qk_w // tn + jb], tn, v_w, BF16, w_t=True)
        o_gate = _mm("plain", h, m_w_t, (j,), [lambda jb: (qk_w + v_w) // tn + jb], tn, v_w, F32, w_t=True)
        gates = _mm("plain", h, m_w_t, (j,), [lambda jb: main_w // LANES + jb], LANES, LANES, F32,
                    w_t=True)[:, :M_GATES * M_HEADS]
        gates = gates + m_gate_b[j].reshape(1, M_GATES * M_HEADS).astype(F32)
        qk_p = _qk_conv(qk_pre, m_conv_w[j], m_conv_b[j], 0, bp, tp, None, dk)
        qk_s = _qk_conv(qk_pre, m_conv_w[j], m_conv_b[j], n_ctx, bs, ts, ts // GRID_W, dk)
        hf_p, hb_p, new_state = _mlstm_scan(qk_p, v, gates, 0, bp, tp, dk, dv, j, None, new_state)
        cached = (state_mlstm_C, state_mlstm_n, state_mlstm_m)
        hf_s, hb_s, _ = _mlstm_scan(qk_s, v, gates, n_ctx, bs, ts, dk, dv, j, cached, None)
        z = jnp.concatenate([_mlstm_combine(hf_p, hb_p, o_gate, m_head_g[j], dv, 0),
                             _mlstm_combine(hf_s, hb_s, o_gate, m_head_g[j], dv, n_ctx)], axis=0)
        x = _mm("resid", z, m_w_out, (j,), [lambda jb: jb], _pick(d, (512, 256, 128)), d, F32,
                res=x, gate=gate_of(l, 1), tm=tm)
        return x, new_state

    def s5(x, l, j):
        seg = _pick(ts, (S5_SEG, 128, 64, 32, 16, 8))
        weights = _s5_weights(s5_lam_re[j], s5_lam_im[j], s5_log_dt[j], s5_b_re[j], s5_b_im[j],
                              s5_c_re[j], s5_c_im[j], seg)
        cached = tuple(jnp.swapaxes(s[:, j], 0, 1).reshape(2, bs, g_groups * p_state).astype(F32)
                       for s in (state_s5_re, state_s5_im))
        z_p, fin = _s5_group(x, norm_g[l, 1], mod, l, s5_w_in, (j,), weights, s5_d[j],
                             0, bp, tp, 0, False, None, True)
        z_s, _ = _s5_group(x, norm_g[l, 1], mod, l, s5_w_in, (j,), weights, s5_d[j],
                           n_ctx, bs, ts, 1, True, cached, False)
        z = jnp.concatenate([z_p, z_s], axis=0)
        tn = _pick(d, (256, 128))
        nd = d // tn
        x = _mm("glu_resid", z, s5_w_out, (j,), [lambda jb: jb, lambda jb: nd + jb], tn, d, F32,
                res=x, gate=gate_of(l, 1), tm=tm)
        fin = tuple(jnp.swapaxes(f, 0, 1).reshape(bp, 2, g_groups, p_state) for f in fin)
        return x, fin

    new_ml = (jnp.zeros((bp, n_ml, 2, M_HEADS, dk, dv), F32), jnp.zeros((bp, n_ml, 2, M_HEADS, dk), F32),
              jnp.zeros((bp, n_ml, 2, M_HEADS), F32))
    new_re, new_im = [], []
    for l in range(depth):
        x = ffn(x, l, 0, 0)
        if l % 2 == 0:
            x, new_ml = mlstm(x, l, l // 2, new_ml)
        else:
            x, (f_re, f_im) = s5(x, l, l // 2)
            new_re.append(f_re)
            new_im.append(f_im)
        x = ffn(x, l, 1, 2)

    y_prompt = _final_norm(x, final_g, 0, n_ctx).reshape(bp, tp, d)
    y_sample = _final_norm(x, final_g, n_ctx, n_lat).reshape(bs, ts, d)
    return (y_prompt, y_sample) + tuple(new_ml) + (jnp.stack(new_re, axis=1), jnp.stack(new_im, axis=1))
```

```python
import functools
import math

import jax
import jax.numpy as jnp
from jax import lax
from jax.experimental import pallas as pl
from jax.experimental.pallas import tpu as pltpu

F32 = jnp.float32
BF16 = jnp.bfloat16

NORM_EPS = 1e-6
N_SUB = 3
N_MOD = 3
GRID_W = 64
M_HEADS = 8
M_GATES = 4
M_SCAN_CHUNK = 256
M_HEADS_PER_STEP = 4
CONV_BLOCK_ROWS = 1024
S5_SB_GROUPS = 16
S5_ROWS = 8
S5_SEG = 256
S5_DOT_SLICE = 256

V7X_VMEM_LIMIT_BYTES = 60000 * 1024
LANES = 128
SUBLANES = 8
ROW_TILES = (1024, 512, 256, 128, 64, 32, 16, 8)


def _pick(n, prefs):
    for p in prefs:
        if p <= n and n % p == 0:
            return p
    return n


def _cparams(n_axes, est_bytes):
    limit = int(min(V7X_VMEM_LIMIT_BYTES, max(est_bytes, 16 * 1024 * 1024)))
    return pltpu.CompilerParams(dimension_semantics=("arbitrary",) * n_axes,
                                vmem_limit_bytes=limit)


def _nbytes(shape, dtype):
    return math.prod(shape) * jnp.dtype(dtype).itemsize


def _ada_body(n_cond, sb_ref, w_ref, b_ref, o_ref):
    k_dim, tn = w_ref.shape
    nl = tn // LANES

    def step(kc, acc):
        k0 = pl.multiple_of(kc * SUBLANES, SUBLANES)
        s = [sb_ref[c, pl.ds(k0, SUBLANES), :] for c in range(n_cond)]
        ws = [w_ref[pl.ds(k0, SUBLANES), l * LANES:(l + 1) * LANES] for l in range(nl)]
        return tuple(acc[c * nl + l] + ws[l] * s[c] for c in range(n_cond) for l in range(nl))

    acc0 = tuple(jnp.zeros((SUBLANES, LANES), F32) for _ in range(n_cond * nl))
    acc = lax.fori_loop(0, k_dim // SUBLANES, step, acc0, unroll=4)
    o_ref[...] = jnp.zeros(o_ref.shape, F32)
    for c in range(n_cond):
        for l in range(nl):
            o_ref[c:c + 1, l * LANES:(l + 1) * LANES] = (
                jnp.sum(acc[c * nl + l], axis=0, keepdims=True) + b_ref[:, l * LANES:(l + 1) * LANES])


def _ada(cond, w_ada, b_ada):
    n_cond, d = cond.shape
    depth, _, n_out = w_ada.shape
    sc = cond * jax.nn.sigmoid(cond)
    sb = jnp.broadcast_to(sc[:, :, None], (n_cond, d, LANES))
    tn = _pick(n_out, (1024, 512, 256, 128))
    est = 2 * (_nbytes((d, tn), F32) + _nbytes((n_cond, d, LANES), F32)) + (4 << 20)
    out = pl.pallas_call(
        functools.partial(_ada_body, n_cond),
        grid=(depth, n_out // tn),
        in_specs=[pl.BlockSpec((n_cond, d, LANES), lambda l, j: (0, 0, 0)),
                  pl.BlockSpec((None, d, tn), lambda l, j: (l, 0, j)),
                  pl.BlockSpec((None, 1, tn), lambda l, j: (l, 0, j))],
        out_specs=pl.BlockSpec((None, SUBLANES, tn), lambda l, j: (l, 0, j)),
        out_shape=jax.ShapeDtypeStruct((depth, SUBLANES, n_out), F32),
        compiler_params=_cparams(2, est),
        name="ada_mod",
    )(sb, w_ada, b_ada.reshape(depth, 1, n_out))
    return out[:, :n_cond]


def _norm_rows(x, g):
    ms = jnp.mean(x * x, axis=-1, keepdims=True)
    return x * lax.rsqrt(ms + NORM_EPS) * g


def _norm_mod_body(x_ref, g_ref, sh_ref, sc_ref, o_ref):
    y = _norm_rows(x_ref[...], g_ref[...])
    o_ref[...] = (y * (1.0 + sc_ref[...]) + sh_ref[...]).astype(o_ref.dtype)


def _norm_body(x_ref, g_ref, o_ref):
    o_ref[...] = _norm_rows(x_ref[...], g_ref[...]).astype(o_ref.dtype)


class _Rows:
    def __init__(self, n_ctx, n_lat_seq, lat_len):
        self.n_ctx = n_ctx
        self.lat_len = lat_len
        self.n = n_ctx + n_lat_seq * lat_len

    def tile(self, prefs=ROW_TILES):
        return _pick(math.gcd(self.n_ctx, self.lat_len), prefs)

    def cond_of_tile(self, i, tm):
        assert self.n_ctx % tm == 0 and self.lat_len % tm == 0
        pt = self.n_ctx // tm
        st = self.lat_len // tm
        return jnp.where(i < pt, 0, 1 + (i - pt) // st)


def _mod_spec(rows, tm, l, slot, tn):
    def imap(i, *rest):
        return (l, rows.cond_of_tile(i, tm), slot, 0, rest[0] if rest else 0)
    return pl.BlockSpec((None, None, None, 1, tn), imap)


def _norm_mod(x, g, mod, rows, l, sub):
    n, d = x.shape
    tm = rows.tile((256, 128, 64, 32, 16, 8))
    est = 2 * (_nbytes((tm, d), F32) + _nbytes((tm, d), BF16)) + 4 * _nbytes((tm, d), F32)
    return pl.pallas_call(
        _norm_mod_body,
        grid=(n // tm,),
        in_specs=[pl.BlockSpec((tm, d), lambda i: (i, 0)),
                  pl.BlockSpec((1, d), lambda i: (0, 0)),
                  _mod_spec(rows, tm, l, sub * N_MOD + 0, d),
                  _mod_spec(rows, tm, l, sub * N_MOD + 1, d)],
        out_specs=pl.BlockSpec((tm, d), lambda i: (i, 0)),
        out_shape=jax.ShapeDtypeStruct((n, d), BF16),
        compiler_params=_cparams(1, est),
        name="norm_mod",
    )(x, g.reshape(1, d), mod, mod)


def _norm_mod_tm(x, g, mod, l, sub, row_off, n_seq, seq_len, n_pad, cond_of_seq):
    n, d = x.shape
    tt = _pick(seq_len, (256, 128, 64, 32, 16, 8))
    nt = seq_len // tt

    def body(x_ref, g_ref, sh_ref, sc_ref, o_ref):
        b = pl.program_id(0)

        @pl.when(b < n_seq)
        def _():
            _norm_mod_body(x_ref, g_ref, sh_ref, sc_ref, o_ref)

        @pl.when(b >= n_seq)
        def _():
            o_ref[...] = jnp.zeros(o_ref.shape, o_ref.dtype)

    def x_map(b, t):
        return (row_off // tt + jnp.minimum(b, n_seq - 1) * nt + t, 0)

    def mod_spec(slot):
        return pl.BlockSpec((None, None, None, 1, d),
                            lambda b, t: (l, cond_of_seq(jnp.minimum(b, n_seq - 1)), slot, 0, 0))

    est = 2 * (_nbytes((tt, d), F32) + _nbytes((tt, d), BF16)) + 4 * _nbytes((tt, d), F32)
    return pl.pallas_call(
        body,
        grid=(n_pad, nt),
        in_specs=[pl.BlockSpec((tt, d), x_map),
                  pl.BlockSpec((1, d), lambda b, t: (0, 0)),
                  mod_spec(sub * N_MOD + 0),
                  mod_spec(sub * N_MOD + 1)],
        out_specs=pl.BlockSpec((tt, d), lambda b, t: (t, b)),
        out_shape=jax.ShapeDtypeStruct((seq_len, n_pad * d), BF16),
        compiler_params=_cparams(2, est),
        name="norm_mod_tm",
    )(x, g.reshape(1, d), mod, mod)


def _final_norm(x, g, row_off, n_rows):
    n, d = x.shape
    tm = _pick(math.gcd(row_off, n_rows) if row_off else n_rows, (256, 128, 64, 32, 16, 8))
    est = 8 * _nbytes((tm, d), F32)
    return pl.pallas_call(
        _norm_body,
        grid=(n_rows // tm,),
        in_specs=[pl.BlockSpec((tm, d), lambda i: (row_off // tm + i, 0)),
                  pl.BlockSpec((1, d), lambda i: (0, 0))],
        out_specs=pl.BlockSpec((tm, d), lambda i: (i, 0)),
        out_shape=jax.ShapeDtypeStruct((n_rows, d), F32),
        compiler_params=_cparams(1, est),
        name="final_norm",
    )(x, g.reshape(1, d))


def _dot(a, b):
    return jnp.dot(a, b, preferred_element_type=F32)


def _mm_body(kind, coef, n_w, has_side, w_t, *refs):
    a_ref, w_refs, rest = refs[0], refs[1:1 + n_w], refs[1 + n_w:]
    if has_side:
        side_in, rest, side_out = rest[0], rest[1:-1], rest[-1]
        side_out[...] = side_in[...].astype(BF16)
    ws = [w[...].astype(BF16) for w in w_refs]
    if w_t:
        acc = lax.dot_general(a_ref[...], ws[0], (((1,), (1,)), ((), ())), preferred_element_type=F32)
        tn = w_refs[0].shape[0]
    else:
        acc = _dot(a_ref[...], ws[0] if n_w == 1 else jnp.concatenate(ws, axis=1))
        tn = w_refs[0].shape[1]
    if kind == "plain":
        (o_ref,) = rest
        o_ref[...] = acc.astype(o_ref.dtype)
    elif kind == "swiglu":
        (o_ref,) = rest
        g, u = acc[:, :tn], acc[:, tn:]
        o_ref[...] = (g * jax.nn.sigmoid(g) * u).astype(o_ref.dtype)
    elif kind == "resid":
        res_ref, gate_ref, o_ref = rest
        o_ref[...] = res_ref[...] + (coef * gate_ref[...]) * acc
    elif kind == "glu_resid":
        res_ref, gate_ref, o_ref = rest
        o_ref[...] = res_ref[...] + gate_ref[...] * (acc[:, :tn] * jax.nn.sigmoid(acc[:, tn:]))
    else:
        raise ValueError(kind)


def _mm(kind, a, w, lead, col_blocks, tn, n_out, out_dtype, *, k_part=(0, 1), coef=1.0,
        res=None, gate=None, tm=None, side=None, w_t=False):
    m = a.shape[0]
    kp = w.shape[-1 if w_t else -2] // k_part[1]
    n_w = len(col_blocks)
    assert not (w_t and n_w > 1)
    if tm is None:
        tm = _pick(m, ROW_TILES)
    n_i, n_j = m // tm, n_out // tn
    n_lead = len(lead)
    in_specs = [pl.BlockSpec((tm, kp), lambda i, j: (i, k_part[0]))]
    if w_t:
        in_specs += [pl.BlockSpec((None,) * n_lead + (tn, kp),
                                  lambda i, j, cb=cb: tuple(lead) + (cb(j), k_part[0]))
                     for cb in col_blocks]
    else:
        in_specs += [pl.BlockSpec((None,) * n_lead + (kp, tn),
                                  lambda i, j, cb=cb: tuple(lead) + (k_part[0], cb(j)))
                     for cb in col_blocks]
    args = [a] + [w] * n_w
    out_specs = [pl.BlockSpec((tm, tn), lambda i, j: (i, j))]
    out_shape = [jax.ShapeDtypeStruct((m, n_out), out_dtype)]
    side_bytes = 0
    if side is not None:
        s_arr, s_lead = side
        s_rows, s_cols = s_arr.shape[-2:]
        assert s_rows % n_j == 0 and s_cols % n_i == 0
        sblk = (s_rows // n_j, s_cols // n_i)
        in_specs.append(pl.BlockSpec((None,) * len(s_lead) + sblk, lambda i, j: tuple(s_lead) + (j, i)))
        args.append(s_arr)
        out_specs.append(pl.BlockSpec(sblk, lambda i, j: (j, i)))
        out_shape.append(jax.ShapeDtypeStruct((s_rows, s_cols), BF16))
        side_bytes = 2 * (_nbytes(sblk, F32) + _nbytes(sblk, BF16))
    if res is not None:
        mod, rows, l, slot = gate
        in_specs += [pl.BlockSpec((tm, tn), lambda i, j: (i, j)), _mod_spec(rows, tm, l, slot, tn)]
        args += [res, mod]
    est = (2 * (_nbytes((tm, kp), BF16) + n_w * _nbytes((kp, tn), w.dtype) + _nbytes((tm, tn), out_dtype))
           + n_w * _nbytes((kp, tn), BF16) + 3 * n_w * _nbytes((tm, tn), F32)
           + (4 * _nbytes((tm, tn), F32) if res is not None else 0) + side_bytes + (2 << 20))
    outs = pl.pallas_call(
        functools.partial(_mm_body, kind, coef, n_w, side is not None, w_t),
        grid=(n_i, n_j),
        in_specs=in_specs,
        out_specs=out_specs,
        out_shape=out_shape,
        compiler_params=_cparams(2, est),
        name="mm_" + kind,
    )(*args)
    return outs if side is not None else outs[0]


def _side_cast_row_tile(m, n_cols):
    for tm in (1280, 1024, 640, 512, 256, 128, 64, 32, 16):
        if m % tm == 0 and n_cols % (m // tm) == 0 and (n_cols // (m // tm)) % LANES == 0:
            return tm
    return None


def _conv_body(grid_rows, seq_len, n_q_blocks, q_scale, x_ref, w_ref, b_ref, o_ref):
    x = x_ref[...]
    t_len = x.shape[0]
    t_idx = lax.broadcasted_iota(jnp.int32, x.shape, 0)
    if t_len > seq_len:
        t_idx = t_idx % seq_len
    if grid_rows is None:
        taps = [(0, dc) for dc in (-1, 0, 1)]
        col = t_idx
        n_cols = seq_len
        row = jnp.zeros_like(t_idx)
        n_rows = 1
    else:
        taps = [(dr, dc) for dr in (-1, 0, 1) for dc in (-1, 0, 1)]
        col = t_idx % GRID_W
        n_cols = GRID_W
        row = t_idx // GRID_W
        n_rows = grid_rows
    acc = jnp.zeros(x.shape, F32)
    for dr, dc in taps:
        off = dr * n_cols + dc
        xs = x if off == 0 else pltpu.roll(x, shift=(-off) % t_len, axis=0)
        ok = ((col + dc >= 0) & (col + dc < n_cols) & (row + dr >= 0) & (row + dr < n_rows))
        wv = w_ref[dr + 1, dc + 1:dc + 2, :]
        acc = acc + jnp.where(ok, xs, 0.0) * wv
    y = acc + b_ref[...]
    y = y * jax.nn.sigmoid(y)
    scale = jnp.where(pl.program_id(1) < n_q_blocks, q_scale, 1.0)
    o_ref[...] = (y * scale).astype(o_ref.dtype)


def _qk_conv(proj, conv_w, conv_b, row_off, n_seq, seq_len, grid_rows, dk):
    ch = conv_w.shape[-1]
    tc = _pick(ch // 2, (512, 256, 128))
    n_q_blocks = (ch // 2) // tc
    nb = 1
    while n_seq % (2 * nb) == 0 and 2 * nb * seq_len <= CONV_BLOCK_ROWS and row_off % (2 * nb * seq_len) == 0:
        nb *= 2
    rows_b = nb * seq_len
    body = functools.partial(_conv_body, grid_rows, seq_len, n_q_blocks, float(dk) ** -0.5)
    est = 2 * (_nbytes((rows_b, tc), F32) + _nbytes((rows_b, tc), BF16)) + 8 * _nbytes((rows_b, tc), F32)
    return pl.pallas_call(
        body,
        grid=(n_seq // nb, ch // tc),
        in_specs=[pl.BlockSpec((rows_b, tc), lambda b, c: (row_off // rows_b + b, c)),
                  pl.BlockSpec((3, 3, tc), lambda b, c: (0, 0, c)),
                  pl.BlockSpec((1, tc), lambda b, c: (0, c))],
        out_specs=pl.BlockSpec((rows_b, tc), lambda b, c: (b, c)),
        out_shape=jax.ShapeDtypeStruct((n_seq * seq_len, ch), BF16),
        compiler_params=_cparams(2, est),
        name="qk_conv",
    )(proj, conv_w, conv_b.reshape(1, ch))


def _log_sigmoid(x):
    return jnp.minimum(x, 0.0) - jnp.log1p(jnp.exp(-jnp.abs(x)))


def _mlstm_decay(backward, li_col, fp_col, li_row, fp_row, m_prev):
    l = li_col.shape[0]
    lf_col = _log_sigmoid(fp_col)
    lf_row = _log_sigmoid(fp_row)
    t_i = lax.broadcasted_iota(jnp.int32, (l, l), 0)
    s_i = lax.broadcasted_iota(jnp.int32, (l, l), 1)
    causal = (s_i >= t_i) if backward else (s_i <= t_i)
    causal_t = (t_i >= s_i) if backward else (t_i <= s_i)
    b_col = jnp.sum(jnp.where(causal, lf_row, 0.0), axis=1, keepdims=True)
    b_row = jnp.sum(jnp.where(causal_t, lf_col, 0.0), axis=0, keepdims=True)
    b_last = jnp.sum(lf_col, axis=0, keepdims=True)
    log_d = jnp.where(causal, b_col - b_row + li_row, -jnp.inf)
    log_inter = b_col + m_prev
    m_out = jnp.maximum(log_inter, jnp.max(log_d, axis=1, keepdims=True))
    dmat = jnp.exp(log_d - m_out)
    s_inter = jnp.exp(log_inter - m_out)
    log_w = b_last - b_col + li_col
    m_new = jnp.maximum(b_last + m_prev, jnp.max(log_w, axis=0, keepdims=True))
    w = jnp.exp(log_w - m_new)
    decay = jnp.exp(b_last + m_prev - m_new)
    return dmat, s_inter, jnp.exp(-m_out), w, decay, m_new


def _mlstm_chunk(chains):
    qk = [lax.dot_general(ch["q"], ch["k"], (((1,), (1,)), ((), ())), preferred_element_type=F32)
          for ch in chains]
    qc = [None if ch["c"] is None else _dot(ch["q"], ch["c"].astype(BF16)) for ch in chains]
    dec = [_mlstm_decay(ch["backward"], ch["li_col"], ch["fp_col"], ch["li_row"], ch["fp_row"], ch["m"])
           for ch in chains]
    s, den, kw, n_new = [], [], [], []
    for ch, (dmat, s_inter, _, w, decay, _), qk_i in zip(chains, dec, qk):
        s_i = qk_i * dmat
        s.append(s_i)
        den_i = jnp.sum(s_i, axis=1, keepdims=True)
        kw_i = ch["k"].astype(F32) * w
        n_i = jnp.sum(kw_i, axis=0, keepdims=True)
        if ch["n"] is not None:
            den_i = den_i + s_inter * jnp.sum(ch["q"].astype(F32) * ch["n"], axis=1, keepdims=True)
            n_i = decay * ch["n"] + n_i
        den.append(den_i)
        kw.append(kw_i)
        n_new.append(n_i)
    sv = [_dot(s_i.astype(BF16), ch["v"]) for ch, s_i in zip(chains, s)]
    kv = [lax.dot_general(kw_i.astype(BF16), ch["v"], (((0,), (0,)), ((), ())), preferred_element_type=F32)
          for ch, kw_i in zip(chains, kw)]
    out = []
    for ch, (_, s_inter, exp_neg_m, _, decay, m_new), qc_i, den_i, sv_i, kv_i, n_i in zip(
            chains, dec, qc, den, sv, kv, n_new):
        num = sv_i if qc_i is None else sv_i + s_inter * qc_i
        h = num / jnp.maximum(jnp.abs(den_i), exp_neg_m)
        out.append((h, kv_i if ch["c"] is None else decay * ch["c"] + kv_i, n_i, m_new))
    return out


def _mlstm_body(zero_init, emit_state, single_chunk, n_heads, dk, dv, *refs):
    it = iter(refs)
    fwd_in = tuple(next(it) for _ in range(5))
    bwd_in = tuple(next(it) for _ in range(5))
    if not zero_init:
        c0, n0, m0 = next(it), next(it), next(it)
    if emit_state:
        next(it), next(it), next(it)
    hf_o, hb_o = next(it), next(it)
    if emit_state:
        c_o, n_o, m_o = next(it), next(it), next(it)
    carried = not (zero_init and single_chunk)
    if carried:
        c_s, n_s, m_s = next(it), next(it), next(it)
        c_idx = pl.program_id(2)

        @pl.when(c_idx == 0)
        def _():
            if zero_init:
                c_s[...] = jnp.zeros(c_s.shape, F32)
                n_s[...] = jnp.zeros(n_s.shape, F32)
                m_s[...] = jnp.zeros(m_s.shape, F32)
            else:
                c_s[...] = c0[...]
                n_s[...] = n0[...]
                m_s[...] = m0[...]

        n_all, m_all = n_s[...], m_s[...]
    else:
        m_all = jnp.zeros((2, n_heads), F32)
    head_row = lax.broadcasted_iota(jnp.int32, (n_heads, dk), 0)
    m_dir = lax.broadcasted_iota(jnp.int32, m_all.shape, 0)
    m_head = lax.broadcasted_iota(jnp.int32, m_all.shape, 1)
    chains = []
    for d, (q_ref, k_ref, v_ref, gc_ref, gr_ref) in enumerate((fwd_in, bwd_in)):
        for h in range(n_heads):
            gi, gf = (2 * d) * n_heads + h, (2 * d + 1) * n_heads + h
            chains.append(dict(
                backward=(d == 1),
                q=q_ref[:, h * dk:(h + 1) * dk], k=k_ref[:, h * dk:(h + 1) * dk],
                v=v_ref[:, h * dv:(h + 1) * dv],
                li_col=gc_ref[:, gi:gi + 1], fp_col=gc_ref[:, gf:gf + 1],
                li_row=gr_ref[gi:gi + 1, :], fp_row=gr_ref[gf:gf + 1, :],
                c=c_s[d, h] if carried else None, n=n_all[d, h:h + 1, :] if carried else None,
                m=m_all[d:d + 1, h:h + 1]))
    results = _mlstm_chunk(chains)
    m_next = m_all
    n_next = []
    for d, h_o in enumerate((hf_o, hb_o)):
        n_d = jnp.zeros((n_heads, dk), F32)
        for h in range(n_heads):
            hh, c_new, n_new, m_new = results[d * n_heads + h]
            h_o[:, h * dv:(h + 1) * dv] = hh
            if carried:
                c_s[d, h] = c_new
            else:
                c_o[d, h] = c_new
            n_d = jnp.where(head_row == h, n_new, n_d)
            m_next = jnp.where((m_dir == d) & (m_head == h), m_new, m_next)
        n_next.append(n_d)
    if carried:
        for d in range(2):
            n_s[d] = n_next[d]
        m_s[...] = m_next
        if emit_state:
            @pl.when(c_idx == pl.num_programs(2) - 1)
            def _():
                c_o[...] = c_s[...]
                n_o[...] = n_s[...]
                m_o[...] = m_s[...]
    else:
        for d in range(2):
            n_o[d] = n_next[d]
        m_o[...] = m_next


def _mlstm_scan(qk, v, gates, row_off, n_seq, seq_len, dk, dv, layer_j, state_in=None, state_out=None):
    nh = M_HEADS
    hb = _pick(nh, (M_HEADS_PER_STEP, 2, 1))
    n_hg = nh // hb
    lc = _pick(seq_len, (M_SCAN_CHUNK, 128, 64, 32, 16, 8))
    nc = seq_len // lc
    n_rows = n_seq * seq_len
    zero_init = state_in is None
    emit_state = state_out is not None
    carried = not (zero_init and nc == 1)
    assert carried or emit_state
    n_tot = gates.shape[0]
    g5 = gates.reshape(n_tot // lc, lc, M_GATES, n_hg, hb)
    g_col = jnp.transpose(g5, (3, 0, 1, 2, 4)).reshape(n_hg, n_tot // lc, lc, M_GATES * hb)
    g_row = jnp.swapaxes(g_col, 2, 3)
    off_c = row_off // lc

    def specs(cmap):
        return [pl.BlockSpec((lc, hb * dk), lambda b, g, c: (cmap(b, c), g)),
                pl.BlockSpec((lc, hb * dk), lambda b, g, c: (cmap(b, c), n_hg + g)),
                pl.BlockSpec((lc, hb * dv), lambda b, g, c: (off_c + cmap(b, c), g)),
                pl.BlockSpec((None, None, lc, M_GATES * hb), lambda b, g, c: (g, off_c + cmap(b, c), 0, 0)),
                pl.BlockSpec((None, None, M_GATES * hb, lc), lambda b, g, c: (g, off_c + cmap(b, c), 0, 0))]

    def fwd(b, c):
        return b * nc + c

    def bwd(b, c):
        return b * nc + (nc - 1 - c)

    def group_views(state):
        c_a, n_a, m_a = state
        lead = c_a.shape[:3]
        return (c_a.reshape(lead + (n_hg, hb, dk, dv)), n_a.reshape(lead + (n_hg, hb, dk)),
                jnp.swapaxes(m_a.reshape(lead + (n_hg, hb)), 2, 3))

    def state_specs():
        return [pl.BlockSpec((None, None, 2, None, hb, dk, dv), lambda b, g, c: (b, layer_j, 0, g, 0, 0, 0)),
                pl.BlockSpec((None, None, 2, None, hb, dk), lambda b, g, c: (b, layer_j, 0, g, 0, 0)),
                pl.BlockSpec((None, None, None, 2, hb), lambda b, g, c: (b, layer_j, g, 0, 0))]

    in_specs = specs(fwd) + specs(bwd)
    args = [qk, qk, v, g_col, g_row] * 2
    if not zero_init:
        in_specs += state_specs()
        args += list(group_views(state_in))
    aliases = {}
    if emit_state:
        out_views = group_views(state_out)
        for t, s in enumerate(out_views):
            aliases[len(args)] = 2 + t
            in_specs.append(pl.BlockSpec(memory_space=pl.ANY))
            args.append(s)
    out_specs = [pl.BlockSpec((lc, hb * dv), lambda b, g, c: (fwd(b, c), g)),
                 pl.BlockSpec((lc, hb * dv), lambda b, g, c: (bwd(b, c), g))]
    out_shape = [jax.ShapeDtypeStruct((n_rows, nh * dv), F32)] * 2
    if emit_state:
        out_specs += state_specs()
        out_shape += [jax.ShapeDtypeStruct(s.shape, F32) for s in out_views]
    scratch = []
    if carried:
        scratch = [pltpu.VMEM((2, hb, dk, dv), F32), pltpu.VMEM((2, hb, dk), F32), pltpu.VMEM((2, hb), F32)]
    state_bytes = _nbytes((2, hb, dk, dv), F32)
    chain_bytes = 4 * _nbytes((lc, lc), F32) + 4 * _nbytes((lc, dv), F32) + 2 * _nbytes((dk, dv), F32)
    est = (state_bytes * ((1 if carried else 0) + (2 if not zero_init else 0) + (2 if emit_state else 0))
           + 2 * hb * chain_bytes + 8 * _nbytes((lc, hb * dv), F32) + 16 * _nbytes((lc, hb * dk), BF16)
           + (4 << 20))
    outs = pl.pallas_call(
        functools.partial(_mlstm_body, zero_init, emit_state, nc == 1, hb, dk, dv),
        grid=(n_seq, n_hg, nc),
        in_specs=in_specs,
        out_specs=out_specs,
        out_shape=out_shape,
        scratch_shapes=scratch,
        input_output_aliases=aliases,
        compiler_params=_cparams(3, est),
        name="mlstm_scan",
    )(*args)
    new_state = None
    if emit_state:
        c_a, n_a, m_a = state_out
        new_state = (outs[2].reshape(c_a.shape), outs[3].reshape(n_a.shape),
                     jnp.swapaxes(outs[4], 2, 3).reshape(m_a.shape))
    return outs[0], outs[1], new_state


def _mlstm_combine_body(hf_ref, hb_ref, o_ref, g_ref, z_ref):
    hh = _norm_rows(hf_ref[...] + hb_ref[...], g_ref[...])
    z_ref[...] = (hh * jax.nn.sigmoid(o_ref[...])).astype(z_ref.dtype)


def _mlstm_combine(h_f, h_b, o_gate, head_g, dv, row_off):
    n_rows, width = h_f.shape
    tm = _pick(math.gcd(n_rows, row_off) if row_off else n_rows, (512, 256, 128, 64, 32, 16, 8))
    est = 2 * (3 * _nbytes((tm, dv), F32) + _nbytes((tm, dv), BF16)) + 4 * _nbytes((tm, dv), F32)
    return pl.pallas_call(
        _mlstm_combine_body,
        grid=(n_rows // tm, M_HEADS),
        in_specs=[pl.BlockSpec((tm, dv), lambda i, h: (i, h)),
                  pl.BlockSpec((tm, dv), lambda i, h: (i, h)),
                  pl.BlockSpec((tm, dv), lambda i, h: (row_off // tm + i, h)),
                  pl.BlockSpec((1, dv), lambda i, h: (0, h))],
        out_specs=pl.BlockSpec((tm, dv), lambda i, h: (i, h)),
        out_shape=jax.ShapeDtypeStruct((n_rows, width), BF16),
        compiler_params=_cparams(2, est),
        name="mlstm_combine",
    )(h_f, h_b, o_gate, head_g.reshape(1, width))


def _s5_body(zero_init, emit_y, emit_state, seq_len, tc_len, *refs):
    it = iter(refs)
    u_ref, bw_ref, ar_ref, ai_ref = (next(it) for _ in range(4))
    cw_ref = next(it) if emit_y else None
    if not zero_init:
        x0r_ref, x0i_ref = next(it), next(it)
    y_ref = next(it) if emit_y else None
    if emit_state:
        fr_ref, fi_ref = next(it), next(it)
    e_bufs = (next(it), next(it))
    xb_s = next(it) if emit_y else None
    ns = ar_ref.shape[-1]
    cin = u_ref.shape[-1]
    n_tc = seq_len // tc_len
    rows_c = tc_len * S5_ROWS
    pair_rows = 2 * S5_ROWS
    n_pairs = tc_len // 2
    slice_w = S5_DOT_SLICE if (2 * ns) % S5_DOT_SLICE == 0 else 2 * ns
    n_sl = (2 * ns) // slice_w
    stages = [(d, c) for d in range(2) for c in (range(n_tc) if d == 0 else range(n_tc - 1, -1, -1))]

    def u_chunk(c):
        return u_ref[c * tc_len:(c + 1) * tc_len].reshape(rows_c, cin).astype(BF16)

    def e_slice(buf, d, u2, sl):
        buf[:, sl * slice_w:(sl + 1) * slice_w] = _dot(u2, bw_ref[d, :, sl * slice_w:(sl + 1) * slice_w])

    u2 = u_chunk(stages[0][1])
    for sl in range(n_sl):
        e_slice(e_bufs[0], stages[0][0], u2, sl)

    for k, (d, c) in enumerate(stages):
        cur, nxt = e_bufs[k % 2], e_bufs[(k + 1) % 2]
        first_c, last_c = (0, n_tc - 1) if d == 0 else (n_tc - 1, 0)
        if c == first_c:
            ar = jnp.broadcast_to(ar_ref[d], (S5_ROWS, ns))
            ai = jnp.broadcast_to(ai_ref[d], (S5_ROWS, ns))
            if zero_init:
                xr = xi = jnp.zeros((S5_ROWS, ns), F32)
            else:
                xr, xi = x0r_ref[d], x0i_ref[d]
        nxt_stage = stages[k + 1] if k + 1 < len(stages) else None
        if nxt_stage is not None:
            u2n = u_chunk(nxt_stage[1])
        sl_done = 0

        def step(r, xr, xi, cur=cur, ar=ar, ai=ai):
            er = cur[r:r + S5_ROWS, 0:ns]
            ei = cur[r:r + S5_ROWS, ns:2 * ns]
            return ar * xr - ai * xi + er, ar * xi + ai * xr + ei

        for p in range(n_pairs):
            while nxt_stage is not None and sl_done < n_sl and sl_done * n_pairs // n_sl <= p:
                e_slice(nxt, nxt_stage[0], u2n, sl_done)
                sl_done += 1
            q = p if d == 0 else n_pairs - 1 - p
            r_lo = q * pair_rows
            r_hi = r_lo + S5_ROWS
            x1 = step(r_lo if d == 0 else r_hi, xr, xi)
            xr, xi = step(r_hi if d == 0 else r_lo, *x1)
            if emit_y:
                lo, hi = (x1, (xr, xi)) if d == 0 else ((xr, xi), x1)
                r_out = c * rows_c + r_lo
                for part in range(2):
                    col = (2 * d + part) * ns
                    xb_s[r_out:r_out + pair_rows, col:col + ns] = (
                        jnp.concatenate([lo[part], hi[part]], axis=0).astype(BF16))
        while nxt_stage is not None and sl_done < n_sl:
            e_slice(nxt, nxt_stage[0], u2n, sl_done)
            sl_done += 1
        if emit_state and c == last_c:
            fr_ref[d] = xr
            fi_ref[d] = xi

    if emit_y:
        y = _dot(xb_s[...], cw_ref[...])
        y_ref[...] = y.reshape(y_ref.shape)


def _s5_scan(u_tm, bw, a_re, a_im, cw, state, emit_y, emit_state):
    seq_len, n_pad, e_width = u_tm.shape
    n_sb = bw.shape[1]
    cin = e_width // n_sb
    ns = bw.shape[-1] // 2
    tc_len = _pick(seq_len, (64, 32, 16, 8, 4, 2))
    n_bg = n_pad // S5_ROWS
    zero_init = state is None

    def wspec(r, c):
        return pl.BlockSpec((2, None, r, c), lambda bg, sb: (0, sb, 0, 0))

    def sspec():
        return pl.BlockSpec((2, S5_ROWS, ns), lambda bg, sb: (0, bg, sb))

    in_specs = [pl.BlockSpec((seq_len, S5_ROWS, cin), lambda bg, sb: (0, bg, sb)),
                wspec(cin, 2 * ns), wspec(1, ns), wspec(1, ns)]
    args = [u_tm, bw, a_re, a_im]
    if emit_y:
        in_specs.append(pl.BlockSpec((None, 4 * ns, cin), lambda bg, sb: (sb, 0, 0)))
        args.append(cw)
    if not zero_init:
        in_specs += [sspec(), sspec()]
        args += list(state)
    out_specs, out_shape = [], []
    scratch = [pltpu.VMEM((tc_len * S5_ROWS, 2 * ns), F32)] * 2
    if emit_y:
        out_specs.append(pl.BlockSpec((seq_len, S5_ROWS, cin), lambda bg, sb: (0, bg, sb)))
        out_shape.append(jax.ShapeDtypeStruct((seq_len, n_pad, e_width), F32))
        scratch.append(pltpu.VMEM((seq_len * S5_ROWS, 4 * ns), BF16))
    if emit_state:
        out_specs += [sspec(), sspec()]
        out_shape += [jax.ShapeDtypeStruct((2, n_pad, n_sb * ns), F32)] * 2
    est = (3 * _nbytes((tc_len * S5_ROWS, 2 * ns), F32) + _nbytes((seq_len * S5_ROWS, 4 * ns), BF16)
           + 5 * _nbytes((seq_len, S5_ROWS, cin), F32) + 8 * _nbytes((cin, 2 * ns), BF16) + (4 << 20))
    outs = pl.pallas_call(
        functools.partial(_s5_body, zero_init, emit_y, emit_state, seq_len, tc_len),
        grid=(n_bg, n_sb),
        in_specs=in_specs,
        out_specs=out_specs,
        out_shape=out_shape,
        scratch_shapes=scratch,
        compiler_params=_cparams(2, est),
        name="s5_scan",
    )(*args)
    y = outs[0] if emit_y else None
    fin = tuple(outs[-2:]) if emit_state else None
    return y, fin


def _s5_carry_body(n_seq, n_seg, fr_ref, fi_ref, x0r_ref, x0i_ref, pr_ref, pi_ref, or_ref, oi_ref):
    or_ref[...] = jnp.zeros(or_ref.shape, F32)
    oi_ref[...] = jnp.zeros(oi_ref.shape, F32)
    for d in range(2):
        pr, pi = pr_ref[d], pi_ref[d]
        order = range(n_seg) if d == 0 else range(n_seg - 1, -1, -1)
        for b in range(n_seq):
            cr, ci = x0r_ref[d, b:b + 1, :], x0i_ref[d, b:b + 1, :]
            for k in order:
                r = b * n_seg + k
                or_ref[d, r:r + 1, :] = cr
                oi_ref[d, r:r + 1, :] = ci
                fr, fi = fr_ref[d, r:r + 1, :], fi_ref[d, r:r + 1, :]
                cr, ci = pr * cr - pi * ci + fr, pr * ci + pi * cr + fi


def _s5_carry(fin, x0, a_pow, n_seq, n_seg):
    shape = fin[0].shape
    return pl.pallas_call(
        functools.partial(_s5_carry_body, n_seq, n_seg),
        out_shape=[jax.ShapeDtypeStruct(shape, F32)] * 2,
        compiler_params=_cparams(0, 16 * _nbytes(shape, F32)),
        name="s5_carry",
    )(fin[0], fin[1], x0[0], x0[1], a_pow[0], a_pow[1])


def _s5_gelu_body(y_ref, u_ref, d_ref, z_ref, v_s):
    c = math.sqrt(2.0 / math.pi)
    v_s[...] = y_ref[...] + d_ref[...] * u_ref[...]
    for b in range(S5_ROWS):
        v = v_s[:, b, :]
        z = 0.5 * v * (1.0 + jnp.tanh(c * (v + 0.044715 * (v * v * v))))
        z_ref[b] = z.astype(z_ref.dtype)


def _s5_gelu(y, u_tm, d_skip):
    seq_len, n_pad, e_width = u_tm.shape
    tt = _pick(seq_len, (128, 64, 32, 16, 8))
    ec = _pick(e_width, (512, 256, 128))
    blk = (tt, S5_ROWS, ec)
    est = 2 * (2 * _nbytes(blk, F32) + _nbytes(blk, BF16)) + 8 * _nbytes(blk, F32)
    return pl.pallas_call(
        _s5_gelu_body,
        grid=(n_pad // S5_ROWS, seq_len // tt, e_width // ec),
        in_specs=[pl.BlockSpec(blk, lambda bg, t, e: (t, bg, e)),
                  pl.BlockSpec(blk, lambda bg, t, e: (t, bg, e)),
                  pl.BlockSpec((1, ec), lambda bg, t, e: (0, e))],
        out_specs=pl.BlockSpec((S5_ROWS, tt, ec), lambda bg, t, e: (bg, t, e)),
        out_shape=jax.ShapeDtypeStruct((n_pad, seq_len, e_width), BF16),
        scratch_shapes=[pltpu.VMEM(blk, F32)],
        compiler_params=_cparams(3, est),
        name="s5_gelu",
    )(y, u_tm, d_skip.reshape(1, e_width))


def _s5_weights(lam_re, lam_im, log_dt, b_re, b_im, c_re, c_im, seg_len):
    lr = jnp.minimum(lam_re.astype(F32), -1e-4)
    li = lam_im.astype(F32)
    dt = jnp.exp(log_dt.astype(F32))[..., None]
    mag = jnp.exp(lr * dt)
    ar = mag * jnp.cos(li * dt)
    ai = mag * jnp.sin(li * dt)
    den = lr * lr + li * li
    xr = ar - 1.0
    cr = (xr * lr + ai * li) / den
    ci = (ai * lr - xr * li) / den
    bp_re = cr[..., None] * b_re - ci[..., None] * b_im
    bp_im = cr[..., None] * b_im + ci[..., None] * b_re
    n_dir, g, p = lam_re.shape
    gc = b_re.shape[-1]
    sbg = min(S5_SB_GROUPS, g)
    n_sb = g // sbg
    same_group = (lax.broadcasted_iota(jnp.int32, (sbg * gc, sbg * p), 0) // gc
                  == lax.broadcasted_iota(jnp.int32, (sbg * gc, sbg * p), 1) // p)

    def bd_in(bp):
        t = jnp.transpose(bp.reshape(n_dir, n_sb, sbg, p, gc), (0, 1, 4, 2, 3))
        t = jnp.tile(t.reshape(n_dir, n_sb, gc, sbg * p), (1, 1, sbg, 1))
        return jnp.where(same_group, t, 0.0)

    def bd_out(cc):
        t = jnp.transpose(cc.reshape(n_dir, n_sb, sbg, gc, p), (0, 1, 2, 4, 3))
        t = jnp.tile(t.reshape(n_dir, n_sb, sbg * p, gc), (1, 1, 1, sbg))
        return jnp.where(same_group.T, t, 0.0)

    bw = jnp.concatenate([bd_in(bp_re), bd_in(bp_im)], axis=-1).astype(BF16)
    cw = jnp.concatenate([bd_out(c_re.astype(F32)), -bd_out(c_im.astype(F32))], axis=-2).astype(BF16)
    cw = jnp.swapaxes(cw, 0, 1).reshape(n_sb, n_dir * 2 * sbg * p, sbg * gc)
    a_re = ar.reshape(n_dir, n_sb, 1, sbg * p)
    a_im = ai.reshape(n_dir, n_sb, 1, sbg * p)
    pr, pi = jnp.ones_like(ar), jnp.zeros_like(ai)
    br, bi = ar, ai
    e = seg_len
    while e:
        if e & 1:
            pr, pi = pr * br - pi * bi, pr * bi + pi * br
        br, bi = br * br - bi * bi, 2.0 * br * bi
        e >>= 1
    a_pow = (pr.reshape(n_dir, 1, g * p), pi.reshape(n_dir, 1, g * p))
    return bw, a_re, a_im, cw, a_pow


def _s5_group(x, g, mod, l, w_in, lead, weights, d_skip, row_off, n_seq, seq_len, cond0, per_seq_cond,
              state, want_state):
    bw, a_re, a_im, cw, a_pow = weights
    d = x.shape[1]
    seg = _pick(seq_len, (S5_SEG, 128, 64, 32, 16, 8))
    n_seg = seq_len // seg
    n_ps = n_seq * n_seg
    n_pad = -(-n_ps // S5_ROWS) * S5_ROWS

    def cond_of(p):
        return cond0 + (p // n_seg if per_seq_cond else 0)

    h_tm = _norm_mod_tm(x, g, mod, l, 1, row_off, n_ps, seg, n_pad, cond_of)
    e_w = w_in.shape[-1]
    u_tm = _mm("plain", h_tm.reshape(seg * n_pad, d), w_in, lead, [lambda j: j],
               _pick(e_w, (512, 256, 128)), e_w, F32).reshape(seg, n_pad, e_w)

    def pad_rows(s):
        return jnp.pad(s, ((0, 0), (0, n_pad - s.shape[1]), (0, 0)))

    if n_seg == 1:
        start = None if state is None else tuple(pad_rows(s) for s in state)
    else:
        assert not want_state
        _, fin = _s5_scan(u_tm, bw, a_re, a_im, cw, None, emit_y=False, emit_state=True)
        x0 = state if state is not None else tuple(jnp.zeros((2, n_seq, fin[0].shape[-1]), F32) for _ in range(2))
        start = _s5_carry(fin, x0, a_pow, n_seq, n_seg)
    y, fin = _s5_scan(u_tm, bw, a_re, a_im, cw, start, emit_y=True, emit_state=want_state)
    z = _s5_gelu(y, u_tm, d_skip).reshape(n_pad * seg, e_w)[:n_seq * seq_len]
    if want_state:
        fin = tuple(f[:, :n_seq] for f in fin)
    return z, fin


def kernel(x_prompt, x_sample, state_mlstm_C, state_mlstm_n, state_mlstm_m, state_s5_re, state_s5_im, c, c_ctx, w_ada, b_ada, norm_g, final_g, w_ffn_in, w_ffn_out, m_w_in, m_conv_w, m_conv_b, m_gate_b, m_head_g, m_w_out, s5_w_in, s5_lam_re, s5_lam_im, s5_log_dt, s5_b_re, s5_b_im, s5_c_re, s5_c_im, s5_d, s5_w_out):
    bp, tp, d = x_prompt.shape
    bs, ts, _ = x_sample.shape
    depth = w_ada.shape[0]
    n_ctx, n_lat = bp * tp, bs * ts
    rows = _Rows(n_ctx, bs, ts)
    tm = rows.tile()
    x = jnp.concatenate([x_prompt.reshape(n_ctx, d), x_sample.reshape(n_lat, d)], axis=0)

    cond = jnp.concatenate([c_ctx[None, :], c], axis=0).astype(F32)
    mod = _ada(cond, w_ada, b_ada).reshape(depth, 1 + bs, N_SUB * N_MOD, 1, d)

    n_ml, n_s5 = m_w_in.shape[0], s5_w_in.shape[0]
    dk, dv = state_mlstm_C.shape[-2], state_mlstm_C.shape[-1]
    qk_w = 2 * M_HEADS * dk
    v_w = M_HEADS * dv
    main_w = qk_w + 2 * v_w
    m_w_t = jnp.swapaxes(m_w_in, 1, 2)
    g_groups, p_state = state_s5_re.shape[-2], state_s5_re.shape[-1]
    d_ff = w_ffn_out.shape[-2]

    def gate_of(l, sub):
        return (mod, rows, l, sub * N_MOD + 2)

    def ffn(x, l, idx, sub):
        h = _norm_mod(x, norm_g[l, sub], mod, rows, l, sub)
        tn = _pick(d_ff, (256, 128))
        nf = d_ff // tn
        up_cols = [lambda j: j, lambda j: nf + j]
        tm_side = _side_cast_row_tile(rows.n, d)
        if tm_side is not None:
            a, w_out = _mm("swiglu", h, w_ffn_in, (l, idx), up_cols, tn, d_ff, BF16, tm=tm_side,
                           side=(w_ffn_out, (l, idx)))
            w_lead, tn_out = (), _pick(d, (512, 256, 128))
        else:
            a = _mm("swiglu", h, w_ffn_in, (l, idx), up_cols, tn, d_ff, BF16)
            w_out, w_lead, tn_out = w_ffn_out, (l, idx), _pick(d, (256, 128))
        k_parts = 2 if d_ff % (2 * LANES) == 0 and d_ff > 4096 else 1
        for p in range(k_parts):
            x = _mm("resid", a, w_out, w_lead, [lambda j: j], tn_out, d, F32,
                    k_part=(p, k_parts), coef=0.5, res=x, gate=gate_of(l, sub), tm=tm)
        return x

    def mlstm(x, l, j, new_state):
        h = _norm_mod(x, norm_g[l, 1], mod, rows, l, 1)
        tn = _pick(math.gcd(qk_w, v_w), (512, 256, 128))
        qk_pre = _mm("plain", h, m_w_t, (j,), [lambda jb: jb], tn, qk_w, F32, w_t=True)
        v = _mm("plain", h, m_w_t, (j,), [lambda jb: qk_w // tn + jb], tn, v_w, BF16, w_t=True)
        o_gate = _mm("plain", h, m_w_t, (j,), [lambda jb: (qk_w + v_w) // tn + jb], tn, v_w, F32, w_t=True)
        gates = _mm("plain", h, m_w_t, (j,), [lambda jb: main_w // LANES + jb], LANES, LANES, F32,
                    w_t=True)[:, :M_GATES * M_HEADS]
        gates = gates + m_gate_b[j].reshape(1, M_GATES * M_HEADS).astype(F32)
        qk_p = _qk_conv(qk_pre, m_conv_w[j], m_conv_b[j], 0, bp, tp, None, dk)
        qk_s = _qk_conv(qk_pre, m_conv_w[j], m_conv_b[j], n_ctx, bs, ts, ts // GRID_W, dk)
        hf_p, hb_p, new_state = _mlstm_scan(qk_p, v, gates, 0, bp, tp, dk, dv, j, None, new_state)
        cached = (state_mlstm_C, state_mlstm_n, state_mlstm_m)
        hf_s, hb_s, _ = _mlstm_scan(qk_s, v, gates, n_ctx, bs, ts, dk, dv, j, cached, None)
        z = jnp.concatenate([_mlstm_combine(hf_p, hb_p, o_gate, m_head_g[j], dv, 0),
                             _mlstm_combine(hf_s, hb_s, o_gate, m_head_g[j], dv, n_ctx)], axis=0)
        x = _mm("resid", z, m_w_out, (j,), [lambda jb: jb], _pick(d, (512, 256, 128)), d, F32,
                res=x, gate=gate_of(l, 1), tm=tm)
        return x, new_state

    def s5(x, l, j):
        seg = _pick(ts, (S5_SEG, 128, 64, 32, 16, 8))
        weights = _s5_weights(s5_lam_re[j], s5_lam_im[j], s5_log_dt[j], s5_b_re[j], s5_b_im[j],
                              s5_c_re[j], s5_c_im[j], seg)
        cached = tuple(jnp.swapaxes(s[:, j], 0, 1).reshape(2, bs, g_groups * p_state).astype(F32)
                       for s in (state_s5_re, state_s5_im))
        z_p, fin = _s5_group(x, norm_g[l, 1], mod, l, s5_w_in, (j,), weights, s5_d[j],
                             0, bp, tp, 0, False, None, True)
        z_s, _ = _s5_group(x, norm_g[l, 1], mod, l, s5_w_in, (j,), weights, s5_d[j],
                           n_ctx, bs, ts, 1, True, cached, False)
        z = jnp.concatenate([z_p, z_s], axis=0)
        tn = _pick(d, (256, 128))
        nd = d // tn
        x = _mm("glu_resid", z, s5_w_out, (j,), [lambda jb: jb, lambda jb: nd + jb], tn, d, F32,
                res=x, gate=gate_of(l, 1), tm=tm)
        fin = tuple(jnp.swapaxes(f, 0, 1).reshape(bp, 2, g_groups, p_state) for f in fin)
        return x, fin

    new_ml = (jnp.zeros((bp, n_ml, 2, M_HEADS, dk, dv), F32), jnp.zeros((bp, n_ml, 2, M_HEADS, dk), F32),
              jnp.zeros((bp, n_ml, 2, M_HEADS), F32))
    new_re, new_im = [], []
    for l in range(depth):
        x = ffn(x, l, 0, 0)
        if l % 2 == 0:
            x, new_ml = mlstm(x, l, l // 2, new_ml)
        else:
            x, (f_re, f_im) = s5(x, l, l // 2)
            new_re.append(f_re)
            new_im.append(f_im)
        x = ffn(x, l, 1, 2)

    y_prompt = _final_norm(x, final_g, 0, n_ctx).reshape(bp, tp, d)
    y_sample = _final_norm(x, final_g, n_ctx, n_lat).reshape(bs, ts, d)
    return (y_prompt, y_sample) + tuple(new_ml) + (jnp.stack(new_re, axis=1), jnp.stack(new_im, axis=1))
```

```python
import functools
import math

import jax
import jax.numpy as jnp
from jax import lax
from jax.experimental import pallas as pl
from jax.experimental.pallas import tpu as pltpu

F32 = jnp.float32
BF16 = jnp.bfloat16

NORM_EPS = 1e-6
N_SUB = 3
N_MOD = 3
GRID_W = 64
M_HEADS = 8
M_GATES = 4
M_SCAN_CHUNK = 256
M_HEADS_PER_STEP = 4
CONV_BLOCK_ROWS = 1024
S5_SB_GROUPS = 16
S5_ROWS = 8
S5_SEG = 256
S5_DOT_SLICE = 256

V7X_VMEM_LIMIT_BYTES = 60000 * 1024
LANES = 128
SUBLANES = 8
ROW_TILES = (1024, 512, 256, 128, 64, 32, 16, 8)


def _pick(n, prefs):
    for p in prefs:
        if p <= n and n % p == 0:
            return p
    return n


def _cparams(n_axes, est_bytes):
    limit = int(min(V7X_VMEM_LIMIT_BYTES, max(est_bytes, 16 * 1024 * 1024)))
    return pltpu.CompilerParams(dimension_semantics=("arbitrary",) * n_axes,
                                vmem_limit_bytes=limit)


def _nbytes(shape, dtype):
    return math.prod(shape) * jnp.dtype(dtype).itemsize


def _ada_body(n_cond, sb_ref, w_ref, b_ref, o_ref):
    k_dim, tn = w_ref.shape
    nl = tn // LANES

    def step(kc, acc):
        k0 = pl.multiple_of(kc * SUBLANES, SUBLANES)
        s = [sb_ref[c, pl.ds(k0, SUBLANES), :] for c in range(n_cond)]
        ws = [w_ref[pl.ds(k0, SUBLANES), l * LANES:(l + 1) * LANES] for l in range(nl)]
        return tuple(acc[c * nl + l] + ws[l] * s[c] for c in range(n_cond) for l in range(nl))

    acc0 = tuple(jnp.zeros((SUBLANES, LANES), F32) for _ in range(n_cond * nl))
    acc = lax.fori_loop(0, k_dim // SUBLANES, step, acc0, unroll=4)
    o_ref[...] = jnp.zeros(o_ref.shape, F32)
    for c in range(n_cond):
        for l in range(nl):
            o_ref[c:c + 1, l * LANES:(l + 1) * LANES] = (
                jnp.sum(acc[c * nl + l], axis=0, keepdims=True) + b_ref[:, l * LANES:(l + 1) * LANES])


def _ada(cond, w_ada, b_ada):
    n_cond, d = cond.shape
    depth, _, n_out = w_ada.shape
    sc = cond * jax.nn.sigmoid(cond)
    sb = jnp.broadcast_to(sc[:, :, None], (n_cond, d, LANES))
    tn = _pick(n_out, (1024, 512, 256, 128))
    est = 2 * (_nbytes((d, tn), F32) + _nbytes((n_cond, d, LANES), F32)) + (4 << 20)
    out = pl.pallas_call(
        functools.partial(_ada_body, n_cond),
        grid=(depth, n_out // tn),
        in_specs=[pl.BlockSpec((n_cond, d, LANES), lambda l, j: (0, 0, 0)),
                  pl.BlockSpec((None, d, tn), lambda l, j: (l, 0, j)),
                  pl.BlockSpec((None, 1, tn), lambda l, j: (l, 0, j))],
        out_specs=pl.BlockSpec((None, SUBLANES, tn), lambda l, j: (l, 0, j)),
        out_shape=jax.ShapeDtypeStruct((depth, SUBLANES, n_out), F32),
        compiler_params=_cparams(2, est),
        name="ada_mod",
    )(sb, w_ada, b_ada.reshape(depth, 1, n_out))
    return out[:, :n_cond]


def _norm_rows(x, g):
    ms = jnp.mean(x * x, axis=-1, keepdims=True)
    return x * lax.rsqrt(ms + NORM_EPS) * g


def _norm_mod_body(x_ref, g_ref, sh_ref, sc_ref, o_ref):
    y = _norm_rows(x_ref[...], g_ref[...])
    o_ref[...] = (y * (1.0 + sc_ref[...]) + sh_ref[...]).astype(o_ref.dtype)


def _norm_body(x_ref, g_ref, o_ref):
    o_ref[...] = _norm_rows(x_ref[...], g_ref[...]).astype(o_ref.dtype)


class _Rows:
    def __init__(self, n_ctx, n_lat_seq, lat_len):
        self.n_ctx = n_ctx
        self.lat_len = lat_len
        self.n = n_ctx + n_lat_seq * lat_len

    def tile(self, prefs=ROW_TILES):
        return _pick(math.gcd(self.n_ctx, self.lat_len), prefs)

    def cond_of_tile(self, i, tm):
        assert self.n_ctx % tm == 0 and self.lat_len % tm == 0
        pt = self.n_ctx // tm
        st = self.lat_len // tm
        return jnp.where(i < pt, 0, 1 + (i - pt) // st)


def _mod_spec(rows, tm, l, slot, tn):
    def imap(i, *rest):
        return (l, rows.cond_of_tile(i, tm), slot, 0, rest[0] if rest else 0)
    return pl.BlockSpec((None, None, None, 1, tn), imap)


def _norm_mod(x, g, mod, rows, l, sub):
    n, d = x.shape
    tm = rows.tile((256, 128, 64, 32, 16, 8))
    est = 2 * (_nbytes((tm, d), F32) + _nbytes((tm, d), BF16)) + 4 * _nbytes((tm, d), F32)
    return pl.pallas_call(
        _norm_mod_body,
        grid=(n // tm,),
        in_specs=[pl.BlockSpec((tm, d), lambda i: (i, 0)),
                  pl.BlockSpec((1, d), lambda i: (0, 0)),
                  _mod_spec(rows, tm, l, sub * N_MOD + 0, d),
                  _mod_spec(rows, tm, l, sub * N_MOD + 1, d)],
        out_specs=pl.BlockSpec((tm, d), lambda i: (i, 0)),
        out_shape=jax.ShapeDtypeStruct((n, d), BF16),
        compiler_params=_cparams(1, est),
        name="norm_mod",
    )(x, g.reshape(1, d), mod, mod)


def _norm_mod_tm(x, g, mod, l, sub, row_off, n_seq, seq_len, n_pad, cond_of_seq):
    n, d = x.shape
    tt = _pick(seq_len, (256, 128, 64, 32, 16, 8))
    nt = seq_len // tt

    def body(x_ref, g_ref, sh_ref, sc_ref, o_ref):
        b = pl.program_id(0)

        @pl.when(b < n_seq)
        def _():
            _norm_mod_body(x_ref, g_ref, sh_ref, sc_ref, o_ref)

        @pl.when(b >= n_seq)
        def _():
            o_ref[...] = jnp.zeros(o_ref.shape, o_ref.dtype)

    def x_map(b, t):
        return (row_off // tt + jnp.minimum(b, n_seq - 1) * nt + t, 0)

    def mod_spec(slot):
        return pl.BlockSpec((None, None, None, 1, d),
                            lambda b, t: (l, cond_of_seq(jnp.minimum(b, n_seq - 1)), slot, 0, 0))

    est = 2 * (_nbytes((tt, d), F32) + _nbytes((tt, d), BF16)) + 4 * _nbytes((tt, d), F32)
    return pl.pallas_call(
        body,
        grid=(n_pad, nt),
        in_specs=[pl.BlockSpec((tt, d), x_map),
                  pl.BlockSpec((1, d), lambda b, t: (0, 0)),
                  mod_spec(sub * N_MOD + 0),
                  mod_spec(sub * N_MOD + 1)],
        out_specs=pl.BlockSpec((tt, d), lambda b, t: (t, b)),
        out_shape=jax.ShapeDtypeStruct((seq_len, n_pad * d), BF16),
        compiler_params=_cparams(2, est),
        name="norm_mod_tm",
    )(x, g.reshape(1, d), mod, mod)


def _final_norm(x, g, row_off, n_rows):
    n, d = x.shape
    tm = _pick(math.gcd(row_off, n_rows) if row_off else n_rows, (256, 128, 64, 32, 16, 8))
    est = 8 * _nbytes((tm, d), F32)
    return pl.pallas_call(
        _norm_body,
        grid=(n_rows // tm,),
        in_specs=[pl.BlockSpec((tm, d), lambda i: (row_off // tm + i, 0)),
                  pl.BlockSpec((1, d), lambda i: (0, 0))],
        out_specs=pl.BlockSpec((tm, d), lambda i: (i, 0)),
        out_shape=jax.ShapeDtypeStruct((n_rows, d), F32),
        compiler_params=_cparams(1, est),
        name="final_norm",
    )(x, g.reshape(1, d))


def _dot(a, b):
    return jnp.dot(a, b, preferred_element_type=F32)


def _mm_body(kind, coef, n_w, has_side, w_t, *refs):
    a_ref, w_refs, rest = refs[0], refs[1:1 + n_w], refs[1 + n_w:]
    if has_side:
        side_in, rest, side_out = rest[0], rest[1:-1], rest[-1]
        side_out[...] = side_in[...].astype(BF16)
    ws = [w[...].astype(BF16) for w in w_refs]
    if w_t:
        acc = lax.dot_general(a_ref[...], ws[0], (((1,), (1,)), ((), ())), preferred_element_type=F32)
        tn = w_refs[0].shape[0]
    else:
        acc = _dot(a_ref[...], ws[0] if n_w == 1 else jnp.concatenate(ws, axis=1))
        tn = w_refs[0].shape[1]
    if kind == "plain":
        (o_ref,) = rest
        o_ref[...] = acc.astype(o_ref.dtype)
    elif kind == "swiglu":
        (o_ref,) = rest
        g, u = acc[:, :tn], acc[:, tn:]
        o_ref[...] = (g * jax.nn.sigmoid(g) * u).astype(o_ref.dtype)
    elif kind == "resid":
        res_ref, gate_ref, o_ref = rest
        o_ref[...] = res_ref[...] + (coef * gate_ref[...]) * acc
    elif kind == "glu_resid":
        res_ref, gate_ref, o_ref = rest
        o_ref[...] = res_ref[...] + gate_ref[...] * (acc[:, :tn] * jax.nn.sigmoid(acc[:, tn:]))
    else:
        raise ValueError(kind)


def _mm(kind, a, w, lead, col_blocks, tn, n_out, out_dtype, *, k_part=(0, 1), coef=1.0,
        res=None, gate=None, tm=None, side=None, w_t=False):
    m = a.shape[0]
    kp = w.shape[-1 if w_t else -2] // k_part[1]
    n_w = len(col_blocks)
    assert not (w_t and n_w > 1)
    if tm is None:
        tm = _pick(m, ROW_TILES)
    n_i, n_j = m // tm, n_out // tn
    n_lead = len(lead)
    in_specs = [pl.BlockSpec((tm, kp), lambda i, j: (i, k_part[0]))]
    if w_t:
        in_specs += [pl.BlockSpec((None,) * n_lead + (tn, kp),
                                  lambda i, j, cb=cb: tuple(lead) + (cb(j), k_part[0]))
                     for cb in col_blocks]
    else:
        in_specs += [pl.BlockSpec((None,) * n_lead + (kp, tn),
                                  lambda i, j, cb=cb: tuple(lead) + (k_part[0], cb(j)))
                     for cb in col_blocks]
    args = [a] + [w] * n_w
    out_specs = [pl.BlockSpec((tm, tn), lambda i, j: (i, j))]
    out_shape = [jax.ShapeDtypeStruct((m, n_out), out_dtype)]
    side_bytes = 0
    if side is not None:
        s_arr, s_lead = side
        s_rows, s_cols = s_arr.shape[-2:]
        assert s_rows % n_j == 0 and s_cols % n_i == 0
        sblk = (s_rows // n_j, s_cols // n_i)
        in_specs.append(pl.BlockSpec((None,) * len(s_lead) + sblk, lambda i, j: tuple(s_lead) + (j, i)))
        args.append(s_arr)
        out_specs.append(pl.BlockSpec(sblk, lambda i, j: (j, i)))
        out_shape.append(jax.ShapeDtypeStruct((s_rows, s_cols), BF16))
        side_bytes = 2 * (_nbytes(sblk, F32) + _nbytes(sblk, BF16))
    if res is not None:
        mod, rows, l, slot = gate
        in_specs += [pl.BlockSpec((tm, tn), lambda i, j: (i, j)), _mod_spec(rows, tm, l, slot, tn)]
        args += [res, mod]
    est = (2 * (_nbytes((tm, kp), BF16) + n_w * _nbytes((kp, tn), w.dtype) + _nbytes((tm, tn), out_dtype))
           + n_w * _nbytes((kp, tn), BF16) + 3 * n_w * _nbytes((tm, tn), F32)
           + (4 * _nbytes((tm, tn), F32) if res is not None else 0) + side_bytes + (2 << 20))
    outs = pl.pallas_call(
        functools.partial(_mm_body, kind, coef, n_w, side is not None, w_t),
        grid=(n_i, n_j),
        in_specs=in_specs,
        out_specs=out_specs,
        out_shape=out_shape,
        compiler_params=_cparams(2, est),
        name="mm_" + kind,
    )(*args)
    return outs if side is not None else outs[0]


def _side_cast_row_tile(m, n_cols):
    for tm in (1280, 1024, 640, 512, 256, 128, 64, 32, 16):
        if m % tm == 0 and n_cols % (m // tm) == 0 and (n_cols // (m // tm)) % LANES == 0:
            return tm
    return None


def _conv_body(grid_rows, seq_len, n_q_blocks, q_scale, x_ref, w_ref, b_ref, o_ref):
    x = x_ref[...]
    t_len = x.shape[0]
    t_idx = lax.broadcasted_iota(jnp.int32, x.shape, 0)
    if t_len > seq_len:
        t_idx = t_idx % seq_len
    if grid_rows is None:
        taps = [(0, dc) for dc in (-1, 0, 1)]
        col = t_idx
        n_cols = seq_len
        row = jnp.zeros_like(t_idx)
        n_rows = 1
    else:
        taps = [(dr, dc) for dr in (-1, 0, 1) for dc in (-1, 0, 1)]
        col = t_idx % GRID_W
        n_cols = GRID_W
        row = t_idx // GRID_W
        n_rows = grid_rows
    acc = jnp.zeros(x.shape, F32)
    for dr, dc in taps:
        off = dr * n_cols + dc
        xs = x if off == 0 else pltpu.roll(x, shift=(-off) % t_len, axis=0)
        ok = ((col + dc >= 0) & (col + dc < n_cols) & (row + dr >= 0) & (row + dr < n_rows))
        wv = w_ref[dr + 1, dc + 1:dc + 2, :]
        acc = acc + jnp.where(ok, xs, 0.0) * wv
    y = acc + b_ref[...]
    y = y * jax.nn.sigmoid(y)
    scale = jnp.where(pl.program_id(1) < n_q_blocks, q_scale, 1.0)
    o_ref[...] = (y * scale).astype(o_ref.dtype)


def _qk_conv(proj, conv_w, conv_b, row_off, n_seq, seq_len, grid_rows, dk):
    ch = conv_w.shape[-1]
    tc = _pick(ch // 2, (512, 256, 128))
    n_q_blocks = (ch // 2) // tc
    nb = 1
    while n_seq % (2 * nb) == 0 and 2 * nb * seq_len <= CONV_BLOCK_ROWS and row_off % (2 * nb * seq_len) == 0:
        nb *= 2
    rows_b = nb * seq_len
    body = functools.partial(_conv_body, grid_rows, seq_len, n_q_blocks, float(dk) ** -0.5)
    est = 2 * (_nbytes((rows_b, tc), F32) + _nbytes((rows_b, tc), BF16)) + 8 * _nbytes((rows_b, tc), F32)
    return pl.pallas_call(
        body,
        grid=(n_seq // nb, ch // tc),
        in_specs=[pl.BlockSpec((rows_b, tc), lambda b, c: (row_off // rows_b + b, c)),
                  pl.BlockSpec((3, 3, tc), lambda b, c: (0, 0, c)),
                  pl.BlockSpec((1, tc), lambda b, c: (0, c))],
        out_specs=pl.BlockSpec((rows_b, tc), lambda b, c: (b, c)),
        out_shape=jax.ShapeDtypeStruct((n_seq * seq_len, ch), BF16),
        compiler_params=_cparams(2, est),
        name="qk_conv",
    )(proj, conv_w, conv_b.reshape(1, ch))


def _log_sigmoid(x):
    return jnp.minimum(x, 0.0) - jnp.log1p(jnp.exp(-jnp.abs(x)))


def _mlstm_decay(backward, li_col, fp_col, li_row, fp_row, m_prev):
    l = li_col.shape[0]
    lf_col = _log_sigmoid(fp_col)
    lf_row = _log_sigmoid(fp_row)
    t_i = lax.broadcasted_iota(jnp.int32, (l, l), 0)
    s_i = lax.broadcasted_iota(jnp.int32, (l, l), 1)
    causal = (s_i >= t_i) if backward else (s_i <= t_i)
    causal_t = (t_i >= s_i) if backward else (t_i <= s_i)
    b_col = jnp.sum(jnp.where(causal, lf_row, 0.0), axis=1, keepdims=True)
    b_row = jnp.sum(jnp.where(causal_t, lf_col, 0.0), axis=0, keepdims=True)
    b_last = jnp.sum(lf_col, axis=0, keepdims=True)
    log_d = jnp.where(causal, b_col - b_row + li_row, -jnp.inf)
    log_inter = b_col + m_prev
    m_out = jnp.maximum(log_inter, jnp.max(log_d, axis=1, keepdims=True))
    dmat = jnp.exp(log_d - m_out)
    s_inter = jnp.exp(log_inter - m_out)
    log_w = b_last - b_col + li_col
    m_new = jnp.maximum(b_last + m_prev, jnp.max(log_w, axis=0, keepdims=True))
    w = jnp.exp(log_w - m_new)
    decay = jnp.exp(b_last + m_prev - m_new)
    return dmat, s_inter, jnp.exp(-m_out), w, decay, m_new


def _mlstm_chunk(chains):
    qk = [lax.dot_general(ch["q"], ch["k"], (((1,), (1,)), ((), ())), preferred_element_type=F32)
          for ch in chains]
    qc = [None if ch["c"] is None else _dot(ch["q"], ch["c"].astype(BF16)) for ch in chains]
    dec = [_mlstm_decay(ch["backward"], ch["li_col"], ch["fp_col"], ch["li_row"], ch["fp_row"], ch["m"])
           for ch in chains]
    s, den, kw, n_new = [], [], [], []
    for ch, (dmat, s_inter, _, w, decay, _), qk_i in zip(chains, dec, qk):
        s_i = qk_i * dmat
        s.append(s_i)
        den_i = jnp.sum(s_i, axis=1, keepdims=True)
        kw_i = ch["k"].astype(F32) * w
        n_i = jnp.sum(kw_i, axis=0, keepdims=True)
        if ch["n"] is not None:
            den_i = den_i + s_inter * jnp.sum(ch["q"].astype(F32) * ch["n"], axis=1, keepdims=True)
            n_i = decay * ch["n"] + n_i
        den.append(den_i)
        kw.append(kw_i)
        n_new.append(n_i)
    sv = [_dot(s_i.astype(BF16), ch["v"]) for ch, s_i in zip(chains, s)]
    kv = [lax.dot_general(kw_i.astype(BF16), ch["v"], (((0,), (0,)), ((), ())), preferred_element_type=F32)
          for ch, kw_i in zip(chains, kw)]
    out = []
    for ch, (_, s_inter, exp_neg_m, _, decay, m_new), qc_i, den_i, sv_i, kv_i, n_i in zip(
            chains, dec, qc, den, sv, kv, n_new):
        num = sv_i if qc_i is None else sv_i + s_inter * qc_i
        h = num / jnp.maximum(jnp.abs(den_i), exp_neg_m)
        out.append((h, kv_i if ch["c"] is None else decay * ch["c"] + kv_i, n_i, m_new))
    return out


def _mlstm_body(zero_init, emit_state, single_chunk, n_heads, dk, dv, *refs):
    it = iter(refs)
    fwd_in = tuple(next(it) for _ in range(5))
    bwd_in = tuple(next(it) for _ in range(5))
    if not zero_init:
        c0, n0, m0 = next(it), next(it), next(it)
    if emit_state:
        next(it), next(it), next(it)
    hf_o, hb_o = next(it), next(it)
    if emit_state:
        c_o, n_o, m_o = next(it), next(it), next(it)
    carried = not (zero_init and single_chunk)
    if carried:
        c_s, n_s, m_s = next(it), next(it), next(it)
        c_idx = pl.program_id(2)

        @pl.when(c_idx == 0)
        def _():
            if zero_init:
                c_s[...] = jnp.zeros(c_s.shape, F32)
                n_s[...] = jnp.zeros(n_s.shape, F32)
                m_s[...] = jnp.zeros(m_s.shape, F32)
            else:
                c_s[...] = c0[...]
                n_s[...] = n0[...]
                m_s[...] = m0[...]

        n_all, m_all = n_s[...], m_s[...]
    else:
        m_all = jnp.zeros((2, n_heads), F32)
    head_row = lax.broadcasted_iota(jnp.int32, (n_heads, dk), 0)
    m_dir = lax.broadcasted_iota(jnp.int32, m_all.shape, 0)
    m_head = lax.broadcasted_iota(jnp.int32, m_all.shape, 1)
    chains = []
    for d, (q_ref, k_ref, v_ref, gc_ref, gr_ref) in enumerate((fwd_in, bwd_in)):
        for h in range(n_heads):
            gi, gf = (2 * d) * n_heads + h, (2 * d + 1) * n_heads + h
            chains.append(dict(
                backward=(d == 1),
                q=q_ref[:, h * dk:(h + 1) * dk], k=k_ref[:, h * dk:(h + 1) * dk],
                v=v_ref[:, h * dv:(h + 1) * dv],
                li_col=gc_ref[:, gi:gi + 1], fp_col=gc_ref[:, gf:gf + 1],
                li_row=gr_ref[gi:gi + 1, :], fp_row=gr_ref[gf:gf + 1, :],
                c=c_s[d, h] if carried else None, n=n_all[d, h:h + 1, :] if carried else None,
                m=m_all[d:d + 1, h:h + 1]))
    results = _mlstm_chunk(chains)
    m_next = m_all
    n_next = []
    for d, h_o in enumerate((hf_o, hb_o)):
        n_d = jnp.zeros((n_heads, dk), F32)
        for h in range(n_heads):
            hh, c_new, n_new, m_new = results[d * n_heads + h]
            h_o[:, h * dv:(h + 1) * dv] = hh
            if carried:
                c_s[d, h] = c_new
            else:
                c_o[d, h] = c_new
            n_d = jnp.where(head_row == h, n_new, n_d)
            m_next = jnp.where((m_dir == d) & (m_head == h), m_new, m_next)
        n_next.append(n_d)
    if carried:
        for d in range(2):
            n_s[d] = n_next[d]
        m_s[...] = m_next
        if emit_state:
            @pl.when(c_idx == pl.num_programs(2) - 1)
            def _():
                c_o[...] = c_s[...]
                n_o[...] = n_s[...]
                m_o[...] = m_s[...]
    else:
        for d in range(2):
            n_o[d] = n_next[d]
        m_o[...] = m_next


def _mlstm_scan(qk, v, gates, row_off, n_seq, seq_len, dk, dv, layer_j, state_in=None, state_out=None):
    nh = M_HEADS
    hb = _pick(nh, (M_HEADS_PER_STEP, 2, 1))
    n_hg = nh // hb
    lc = _pick(seq_len, (M_SCAN_CHUNK, 128, 64, 32, 16, 8))
    nc = seq_len // lc
    n_rows = n_seq * seq_len
    zero_init = state_in is None
    emit_state = state_out is not None
    carried = not (zero_init and nc == 1)
    assert carried or emit_state
    n_tot = gates.shape[0]
    g5 = gates.reshape(n_tot // lc, lc, M_GATES, n_hg, hb)
    g_col = jnp.transpose(g5, (3, 0, 1, 2, 4)).reshape(n_hg, n_tot // lc, lc, M_GATES * hb)
    g_row = jnp.swapaxes(g_col, 2, 3)
    off_c = row_off // lc

    def specs(cmap):
        return [pl.BlockSpec((lc, hb * dk), lambda b, g, c: (cmap(b, c), g)),
                pl.BlockSpec((lc, hb * dk), lambda b, g, c: (cmap(b, c), n_hg + g)),
                pl.BlockSpec((lc, hb * dv), lambda b, g, c: (off_c + cmap(b, c), g)),
                pl.BlockSpec((None, None, lc, M_GATES * hb), lambda b, g, c: (g, off_c + cmap(b, c), 0, 0)),
                pl.BlockSpec((None, None, M_GATES * hb, lc), lambda b, g, c: (g, off_c + cmap(b, c), 0, 0))]

    def fwd(b, c):
        return b * nc + c

    def bwd(b, c):
        return b * nc + (nc - 1 - c)

    def group_views(state):
        c_a, n_a, m_a = state
        lead = c_a.shape[:3]
        return (c_a.reshape(lead + (n_hg, hb, dk, dv)), n_a.reshape(lead + (n_hg, hb, dk)),
                jnp.swapaxes(m_a.reshape(lead + (n_hg, hb)), 2, 3))

    def state_specs():
        return [pl.BlockSpec((None, None, 2, None, hb, dk, dv), lambda b, g, c: (b, layer_j, 0, g, 0, 0, 0)),
                pl.BlockSpec((None, None, 2, None, hb, dk), lambda b, g, c: (b, layer_j, 0, g, 0, 0)),
                pl.BlockSpec((None, None, None, 2, hb), lambda b, g, c: (b, layer_j, g, 0, 0))]

    in_specs = specs(fwd) + specs(bwd)
    args = [qk, qk, v, g_col, g_row] * 2
    if not zero_init:
        in_specs += state_specs()
        args += list(group_views(state_in))
    aliases = {}
    if emit_state:
        out_views = group_views(state_out)
        for t, s in enumerate(out_views):
            aliases[len(args)] = 2 + t
            in_specs.append(pl.BlockSpec(memory_space=pl.ANY))
            args.append(s)
    out_specs = [pl.BlockSpec((lc, hb * dv), lambda b, g, c: (fwd(b, c), g)),
                 pl.BlockSpec((lc, hb * dv), lambda b, g, c: (bwd(b, c), g))]
    out_shape = [jax.ShapeDtypeStruct((n_rows, nh * dv), F32)] * 2
    if emit_state:
        out_specs += state_specs()
        out_shape += [jax.ShapeDtypeStruct(s.shape, F32) for s in out_views]
    scratch = []
    if carried:
        scratch = [pltpu.VMEM((2, hb, dk, dv), F32), pltpu.VMEM((2, hb, dk), F32), pltpu.VMEM((2, hb), F32)]
    state_bytes = _nbytes((2, hb, dk, dv), F32)
    chain_bytes = 4 * _nbytes((lc, lc), F32) + 4 * _nbytes((lc, dv), F32) + 2 * _nbytes((dk, dv), F32)
    est = (state_bytes * ((1 if carried else 0) + (2 if not zero_init else 0) + (2 if emit_state else 0))
           + 2 * hb * chain_bytes + 8 * _nbytes((lc, hb * dv), F32) + 16 * _nbytes((lc, hb * dk), BF16)
           + (4 << 20))
    outs = pl.pallas_call(
        functools.partial(_mlstm_body, zero_init, emit_state, nc == 1, hb, dk, dv),
        grid=(n_seq, n_hg, nc),
        in_specs=in_specs,
        out_specs=out_specs,
        out_shape=out_shape,
        scratch_shapes=scratch,
        input_output_aliases=aliases,
        compiler_params=_cparams(3, est),
        name="mlstm_scan",
    )(*args)
    new_state = None
    if emit_state:
        c_a, n_a, m_a = state_out
        new_state = (outs[2].reshape(c_a.shape), outs[3].reshape(n_a.shape),
                     jnp.swapaxes(outs[4], 2, 3).reshape(m_a.shape))
    return outs[0], outs[1], new_state


def _mlstm_combine_body(hf_ref, hb_ref, o_ref, g_ref, z_ref):
    hh = _norm_rows(hf_ref[...] + hb_ref[...], g_ref[...])
    z_ref[...] = (hh * jax.nn.sigmoid(o_ref[...])).astype(z_ref.dtype)


def _mlstm_combine(h_f, h_b, o_gate, head_g, dv, row_off):
    n_rows, width = h_f.shape
    tm = _pick(math.gcd(n_rows, row_off) if row_off else n_rows, (512, 256, 128, 64, 32, 16, 8))
    est = 2 * (3 * _nbytes((tm, dv), F32) + _nbytes((tm, dv), BF16)) + 4 * _nbytes((tm, dv), F32)
    return pl.pallas_call(
        _mlstm_combine_body,
        grid=(n_rows // tm, M_HEADS),
        in_specs=[pl.BlockSpec((tm, dv), lambda i, h: (i, h)),
                  pl.BlockSpec((tm, dv), lambda i, h: (i, h)),
                  pl.BlockSpec((tm, dv), lambda i, h: (row_off // tm + i, h)),
                  pl.BlockSpec((1, dv), lambda i, h: (0, h))],
        out_specs=pl.BlockSpec((tm, dv), lambda i, h: (i, h)),
        out_shape=jax.ShapeDtypeStruct((n_rows, width), BF16),
        compiler_params=_cparams(2, est),
        name="mlstm_combine",
    )(h_f, h_b, o_gate, head_g.reshape(1, width))


def _s5_body(zero_init, emit_y, emit_state, seq_len, tc_len, *refs):
    it = iter(refs)
    u_ref, bw_ref, ar_ref, ai_ref = (next(it) for _ in range(4))
    cw_ref = next(it) if emit_y else None
    if not zero_init:
        x0r_ref, x0i_ref = next(it), next(it)
    y_ref = next(it) if emit_y else None
    if emit_state:
        fr_ref, fi_ref = next(it), next(it)
    e_bufs = (next(it), next(it))
    xb_s = next(it) if emit_y else None
    ns = ar_ref.shape[-1]
    cin = u_ref.shape[-1]
    n_tc = seq_len // tc_len
    rows_c = tc_len * S5_ROWS
    pair_rows = 2 * S5_ROWS
    n_pairs = tc_len // 2
    slice_w = S5_DOT_SLICE if (2 * ns) % S5_DOT_SLICE == 0 else 2 * ns
    n_sl = (2 * ns) // slice_w
    stages = [(d, c) for d in range(2) for c in (range(n_tc) if d == 0 else range(n_tc - 1, -1, -1))]

    def u_chunk(c):
        return u_ref[c * tc_len:(c + 1) * tc_len].reshape(rows_c, cin).astype(BF16)

    def e_slice(buf, d, u2, sl):
        buf[:, sl * slice_w:(sl + 1) * slice_w] = _dot(u2, bw_ref[d, :, sl * slice_w:(sl + 1) * slice_w])

    u2 = u_chunk(stages[0][1])
    for sl in range(n_sl):
        e_slice(e_bufs[0], stages[0][0], u2, sl)

    for k, (d, c) in enumerate(stages):
        cur, nxt = e_bufs[k % 2], e_bufs[(k + 1) % 2]
        first_c, last_c = (0, n_tc - 1) if d == 0 else (n_tc - 1, 0)
        if c == first_c:
            ar = jnp.broadcast_to(ar_ref[d], (S5_ROWS, ns))
            ai = jnp.broadcast_to(ai_ref[d], (S5_ROWS, ns))
            if zero_init:
                xr = xi = jnp.zeros((S5_ROWS, ns), F32)
            else:
                xr, xi = x0r_ref[d], x0i_ref[d]
        nxt_stage = stages[k + 1] if k + 1 < len(stages) else None
        if nxt_stage is not None:
            u2n = u_chunk(nxt_stage[1])
        sl_done = 0

        def step(r, xr, xi, cur=cur, ar=ar, ai=ai):
            er = cur[r:r + S5_ROWS, 0:ns]
            ei = cur[r:r + S5_ROWS, ns:2 * ns]
            return ar * xr - ai * xi + er, ar * xi + ai * xr + ei

        for p in range(n_pairs):
            while nxt_stage is not None and sl_done < n_sl and sl_done * n_pairs // n_sl <= p:
                e_slice(nxt, nxt_stage[0], u2n, sl_done)
                sl_done += 1
            q = p if d == 0 else n_pairs - 1 - p
            r_lo = q * pair_rows
            r_hi = r_lo + S5_ROWS
            x1 = step(r_lo if d == 0 else r_hi, xr, xi)
            xr, xi = step(r_hi if d == 0 else r_lo, *x1)
            if emit_y:
                lo, hi = (x1, (xr, xi)) if d == 0 else ((xr, xi), x1)
                r_out = c * rows_c + r_lo
                for part in range(2):
                    col = (2 * d + part) * ns
                    xb_s[r_out:r_out + pair_rows, col:col + ns] = (
                        jnp.concatenate([lo[part], hi[part]], axis=0).astype(BF16))
        while nxt_stage is not None and sl_done < n_sl:
            e_slice(nxt, nxt_stage[0], u2n, sl_done)
            sl_done += 1
        if emit_state and c == last_c:
            fr_ref[d] = xr
            fi_ref[d] = xi

    if emit_y:
        y = _dot(xb_s[...], cw_ref[...])
        y_ref[...] = y.reshape(y_ref.shape)


def _s5_scan(u_tm, bw, a_re, a_im, cw, state, emit_y, emit_state):
    seq_len, n_pad, e_width = u_tm.shape
    n_sb = bw.shape[1]
    cin = e_width // n_sb
    ns = bw.shape[-1] // 2
    tc_len = _pick(seq_len, (64, 32, 16, 8, 4, 2))
    n_bg = n_pad // S5_ROWS
    zero_init = state is None

    def wspec(r, c):
        return pl.BlockSpec((2, None, r, c), lambda bg, sb: (0, sb, 0, 0))

    def sspec():
        return pl.BlockSpec((2, S5_ROWS, ns), lambda bg, sb: (0, bg, sb))

    in_specs = [pl.BlockSpec((seq_len, S5_ROWS, cin), lambda bg, sb: (0, bg, sb)),
                wspec(cin, 2 * ns), wspec(1, ns), wspec(1, ns)]
    args = [u_tm, bw, a_re, a_im]
    if emit_y:
        in_specs.append(pl.BlockSpec((None, 4 * ns, cin), lambda bg, sb: (sb, 0, 0)))
        args.append(cw)
    if not zero_init:
        in_specs += [sspec(), sspec()]
        args += list(state)
    out_specs, out_shape = [], []
    scratch = [pltpu.VMEM((tc_len * S5_ROWS, 2 * ns), F32)] * 2
    if emit_y:
        out_specs.append(pl.BlockSpec((seq_len, S5_ROWS, cin), lambda bg, sb: (0, bg, sb)))
        out_shape.append(jax.ShapeDtypeStruct((seq_len, n_pad, e_width), F32))
        scratch.append(pltpu.VMEM((seq_len * S5_ROWS, 4 * ns), BF16))
    if emit_state:
        out_specs += [sspec(), sspec()]
        out_shape += [jax.ShapeDtypeStruct((2, n_pad, n_sb * ns), F32)] * 2
    est = (3 * _nbytes((tc_len * S5_ROWS, 2 * ns), F32) + _nbytes((seq_len * S5_ROWS, 4 * ns), BF16)
           + 5 * _nbytes((seq_len, S5_ROWS, cin), F32) + 8 * _nbytes((cin, 2 * ns), BF16) + (4 << 20))
    outs = pl.pallas_call(
        functools.partial(_s5_body, zero_init, emit_y, emit_state, seq_len, tc_len),
        grid=(n_bg, n_sb),
        in_specs=in_specs,
        out_specs=out_specs,
        out_shape=out_shape,
        scratch_shapes=scratch,
        compiler_params=_cparams(2, est),
        name="s5_scan",
    )(*args)
    y = outs[0] if emit_y else None
    fin = tuple(outs[-2:]) if emit_state else None
    return y, fin


def _s5_carry_body(n_seq, n_seg, fr_ref, fi_ref, x0r_ref, x0i_ref, pr_ref, pi_ref, or_ref, oi_ref):
    or_ref[...] = jnp.zeros(or_ref.shape, F32)
    oi_ref[...] = jnp.zeros(oi_ref.shape, F32)
    for d in range(2):
        pr, pi = pr_ref[d], pi_ref[d]
        order = range(n_seg) if d == 0 else range(n_seg - 1, -1, -1)
        for b in range(n_seq):
            cr, ci = x0r_ref[d, b:b + 1, :], x0i_ref[d, b:b + 1, :]
            for k in order:
                r = b * n_seg + k
                or_ref[d, r:r + 1, :] = cr
                oi_ref[d, r:r + 1, :] = ci
                fr, fi = fr_ref[d, r:r + 1, :], fi_ref[d, r:r + 1, :]
                cr, ci = pr * cr - pi * ci + fr, pr * ci + pi * cr + fi


def _s5_carry(fin, x0, a_pow, n_seq, n_seg):
    shape = fin[0].shape
    return pl.pallas_call(
        functools.partial(_s5_carry_body, n_seq, n_seg),
        out_shape=[jax.ShapeDtypeStruct(shape, F32)] * 2,
        compiler_params=_cparams(0, 16 * _nbytes(shape, F32)),
        name="s5_carry",
    )(fin[0], fin[1], x0[0], x0[1], a_pow[0], a_pow[1])


def _s5_gelu_body(y_ref, u_ref, d_ref, z_ref, v_s):
    c = math.sqrt(2.0 / math.pi)
    tt, nb, ec = y_ref.shape
    v = (y_ref[...] + d_ref[...] * u_ref[...]).reshape(tt * nb, ec)
    z = (0.5 * v * (1.0 + jnp.tanh(c * (v + 0.044715 * (v * v * v))))).astype(BF16)
    t_i = lax.broadcasted_iota(jnp.int32, (tt, tt * nb), 0)
    r_i = lax.broadcasted_iota(jnp.int32, (tt, tt * nb), 1)
    for b in range(nb):
        sel = jnp.where(r_i == t_i * nb + b, 1.0, 0.0).astype(BF16)
        z_ref[b] = _dot(sel, z).astype(z_ref.dtype)


def _s5_gelu(y, u_tm, d_skip):
    seq_len, n_pad, e_width = u_tm.shape
    tt = _pick(seq_len, (128, 64, 32, 16, 8))
    ec = _pick(e_width, (512, 256, 128))
    blk = (tt, S5_ROWS, ec)
    est = 2 * (2 * _nbytes(blk, F32) + _nbytes(blk, BF16)) + 8 * _nbytes(blk, F32)
    return pl.pallas_call(
        _s5_gelu_body,
        grid=(n_pad // S5_ROWS, seq_len // tt, e_width // ec),
        in_specs=[pl.BlockSpec(blk, lambda bg, t, e: (t, bg, e)),
                  pl.BlockSpec(blk, lambda bg, t, e: (t, bg, e)),
                  pl.BlockSpec((1, ec), lambda bg, t, e: (0, e))],
        out_specs=pl.BlockSpec((S5_ROWS, tt, ec), lambda bg, t, e: (bg, t, e)),
        out_shape=jax.ShapeDtypeStruct((n_pad, seq_len, e_width), BF16),
        scratch_shapes=[pltpu.VMEM(blk, F32)],
        compiler_params=_cparams(3, est),
        name="s5_gelu",
    )(y, u_tm, d_skip.reshape(1, e_width))


def _s5_weights(lam_re, lam_im, log_dt, b_re, b_im, c_re, c_im, seg_len):
    lr = jnp.minimum(lam_re.astype(F32), -1e-4)
    li = lam_im.astype(F32)
    dt = jnp.exp(log_dt.astype(F32))[..., None]
    mag = jnp.exp(lr * dt)
    ar = mag * jnp.cos(li * dt)
    ai = mag * jnp.sin(li * dt)
    den = lr * lr + li * li
    xr = ar - 1.0
    cr = (xr * lr + ai * li) / den
    ci = (ai * lr - xr * li) / den
    bp_re = cr[..., None] * b_re - ci[..., None] * b_im
    bp_im = cr[..., None] * b_im + ci[..., None] * b_re
    n_dir, g, p = lam_re.shape
    gc = b_re.shape[-1]
    sbg = min(S5_SB_GROUPS, g)
    n_sb = g // sbg
    same_group = (lax.broadcasted_iota(jnp.int32, (sbg * gc, sbg * p), 0) // gc
                  == lax.broadcasted_iota(jnp.int32, (sbg * gc, sbg * p), 1) // p)

    def bd_in(bp):
        t = jnp.transpose(bp.reshape(n_dir, n_sb, sbg, p, gc), (0, 1, 4, 2, 3))
        t = jnp.tile(t.reshape(n_dir, n_sb, gc, sbg * p), (1, 1, sbg, 1))
        return jnp.where(same_group, t, 0.0)

    def bd_out(cc):
        t = jnp.transpose(cc.reshape(n_dir, n_sb, sbg, gc, p), (0, 1, 2, 4, 3))
        t = jnp.tile(t.reshape(n_dir, n_sb, sbg * p, gc), (1, 1, 1, sbg))
        return jnp.where(same_group.T, t, 0.0)

    bw = jnp.concatenate([bd_in(bp_re), bd_in(bp_im)], axis=-1).astype(BF16)
    cw = jnp.concatenate([bd_out(c_re.astype(F32)), -bd_out(c_im.astype(F32))], axis=-2).astype(BF16)
    cw = jnp.swapaxes(cw, 0, 1).reshape(n_sb, n_dir * 2 * sbg * p, sbg * gc)
    a_re = ar.reshape(n_dir, n_sb, 1, sbg * p)
    a_im = ai.reshape(n_dir, n_sb, 1, sbg * p)
    pr, pi = jnp.ones_like(ar), jnp.zeros_like(ai)
    br, bi = ar, ai
    e = seg_len
    while e:
        if e & 1:
            pr, pi = pr * br - pi * bi, pr * bi + pi * br
        br, bi = br * br - bi * bi, 2.0 * br * bi
        e >>= 1
    a_pow = (pr.reshape(n_dir, 1, g * p), pi.reshape(n_dir, 1, g * p))
    return bw, a_re, a_im, cw, a_pow


def _s5_group(x, g, mod, l, w_in, lead, weights, d_skip, row_off, n_seq, seq_len, cond0, per_seq_cond,
              state, want_state):
    bw, a_re, a_im, cw, a_pow = weights
    d = x.shape[1]
    seg = _pick(seq_len, (S5_SEG, 128, 64, 32, 16, 8))
    n_seg = seq_len // seg
    n_ps = n_seq * n_seg
    n_pad = -(-n_ps // S5_ROWS) * S5_ROWS

    def cond_of(p):
        return cond0 + (p // n_seg if per_seq_cond else 0)

    h_tm = _norm_mod_tm(x, g, mod, l, 1, row_off, n_ps, seg, n_pad, cond_of)
    e_w = w_in.shape[-1]
    u_tm = _mm("plain", h_tm.reshape(seg * n_pad, d), w_in, lead, [lambda j: j],
               _pick(e_w, (512, 256, 128)), e_w, F32).reshape(seg, n_pad, e_w)

    def pad_rows(s):
        return jnp.pad(s, ((0, 0), (0, n_pad - s.shape[1]), (0, 0)))

    if n_seg == 1:
        start = None if state is None else tuple(pad_rows(s) for s in state)
    else:
        assert not want_state
        _, fin = _s5_scan(u_tm, bw, a_re, a_im, cw, None, emit_y=False, emit_state=True)
        x0 = state if state is not None else tuple(jnp.zeros((2, n_seq, fin[0].shape[-1]), F32) for _ in range(2))
        start = _s5_carry(fin, x0, a_pow, n_seq, n_seg)
    y, fin = _s5_scan(u_tm, bw, a_re, a_im, cw, start, emit_y=True, emit_state=want_state)
    z = _s5_gelu(y, u_tm, d_skip).reshape(n_pad * seg, e_w)[:n_seq * seq_len]
    if want_state:
        fin = tuple(f[:, :n_seq] for f in fin)
    return z, fin


def kernel(x_prompt, x_sample, state_mlstm_C, state_mlstm_n, state_mlstm_m, state_s5_re, state_s5_im, c, c_ctx, w_ada, b_ada, norm_g, final_g, w_ffn_in, w_ffn_out, m_w_in, m_conv_w, m_conv_b, m_gate_b, m_head_g, m_w_out, s5_w_in, s5_lam_re, s5_lam_im, s5_log_dt, s5_b_re, s5_b_im, s5_c_re, s5_c_im, s5_d, s5_w_out):
    bp, tp, d = x_prompt.shape
    bs, ts, _ = x_sample.shape
    depth = w_ada.shape[0]
    n_ctx, n_lat = bp * tp, bs * ts
    rows = _Rows(n_ctx, bs, ts)
    tm = rows.tile()
    x = jnp.concatenate([x_prompt.reshape(n_ctx, d), x_sample.reshape(n_lat, d)], axis=0)

    cond = jnp.concatenate([c_ctx[None, :], c], axis=0).astype(F32)
    mod = _ada(cond, w_ada, b_ada).reshape(depth, 1 + bs, N_SUB * N_MOD, 1, d)

    n_ml, n_s5 = m_w_in.shape[0], s5_w_in.shape[0]
    dk, dv = state_mlstm_C.shape[-2], state_mlstm_C.shape[-1]
    qk_w = 2 * M_HEADS * dk
    v_w = M_HEADS * dv
    main_w = qk_w + 2 * v_w
    m_w_t = jnp.swapaxes(m_w_in, 1, 2)
    g_groups, p_state = state_s5_re.shape[-2], state_s5_re.shape[-1]
    d_ff = w_ffn_out.shape[-2]

    def gate_of(l, sub):
        return (mod, rows, l, sub * N_MOD + 2)

    def ffn(x, l, idx, sub):
        h = _norm_mod(x, norm_g[l, sub], mod, rows, l, sub)
        tn = _pick(d_ff, (256, 128))
        nf = d_ff // tn
        up_cols = [lambda j: j, lambda j: nf + j]
        tm_side = _side_cast_row_tile(rows.n, d)
        if tm_side is not None:
            a, w_out = _mm("swiglu", h, w_ffn_in, (l, idx), up_cols, tn, d_ff, BF16, tm=tm_side,
                           side=(w_ffn_out, (l, idx)))
            w_lead, tn_out = (), _pick(d, (512, 256, 128))
        else:
            a = _mm("swiglu", h, w_ffn_in, (l, idx), up_cols, tn, d_ff, BF16)
            w_out, w_lead, tn_out = w_ffn_out, (l, idx), _pick(d, (256, 128))
        k_parts = 2 if d_ff % (2 * LANES) == 0 and d_ff > 4096 else 1
        for p in range(k_parts):
            x = _mm("resid", a, w_out, w_lead, [lambda j: j], tn_out, d, F32,
                    k_part=(p, k_parts), coef=0.5, res=x, gate=gate_of(l, sub), tm=tm)
        return x

    def mlstm(x, l, j, new_state):
        h = _norm_mod(x, norm_g[l, 1], mod, rows, l, 1)
        tn = _pick(math.gcd(qk_w, v_w), (512, 256, 128))
        qk_pre = _mm("plain", h, m_w_t, (j,), [lambda jb: jb], tn, qk_w, F32, w_t=True)
        v = _mm("plain", h, m_w_t, (j,), [lambda jb: qk_w // tn + jb], tn, v_w, BF16, w_t=True)
        o_gate = _mm("plain", h, m_w_t, (j,), [lambda jb: (qk_w + v_w) // tn + jb], tn, v_w, F32, w_t=True)
        gates = _mm("plain", h, m_w_t, (j,), [lambda jb: main_w // LANES + jb], LANES, LANES, F32,
                    w_t=True)[:, :M_GATES * M_HEADS]
        gates = gates + m_gate_b[j].reshape(1, M_GATES * M_HEADS).astype(F32)
        qk_p = _qk_conv(qk_pre, m_conv_w[j], m_conv_b[j], 0, bp, tp, None, dk)
        qk_s = _qk_conv(qk_pre, m_conv_w[j], m_conv_b[j], n_ctx, bs, ts, ts // GRID_W, dk)
        hf_p, hb_p, new_state = _mlstm_scan(qk_p, v, gates, 0, bp, tp, dk, dv, j, None, new_state)
        cached = (state_mlstm_C, state_mlstm_n, state_mlstm_m)
        hf_s, hb_s, _ = _mlstm_scan(qk_s, v, gates, n_ctx, bs, ts, dk, dv, j, cached, None)
        z = jnp.concatenate([_mlstm_combine(hf_p, hb_p, o_gate, m_head_g[j], dv, 0),
                             _mlstm_combine(hf_s, hb_s, o_gate, m_head_g[j], dv, n_ctx)], axis=0)
        x = _mm("resid", z, m_w_out, (j,), [lambda jb: jb], _pick(d, (512, 256, 128)), d, F32,
                res=x, gate=gate_of(l, 1), tm=tm)
        return x, new_state

    def s5(x, l, j):
        seg = _pick(ts, (S5_SEG, 128, 64, 32, 16, 8))
        weights = _s5_weights(s5_lam_re[j], s5_lam_im[j], s5_log_dt[j], s5_b_re[j], s5_b_im[j],
                              s5_c_re[j], s5_c_im[j], seg)
        cached = tuple(jnp.swapaxes(s[:, j], 0, 1).reshape(2, bs, g_groups * p_state).astype(F32)
                       for s in (state_s5_re, state_s5_im))
        z_p, fin = _s5_group(x, norm_g[l, 1], mod, l, s5_w_in, (j,), weights, s5_d[j],
                             0, bp, tp, 0, False, None, True)
        z_s, _ = _s5_group(x, norm_g[l, 1], mod, l, s5_w_in, (j,), weights, s5_d[j],
                           n_ctx, bs, ts, 1, True, cached, False)
        z = jnp.concatenate([z_p, z_s], axis=0)
        tn = _pick(d, (256, 128))
        nd = d // tn
        x = _mm("glu_resid", z, s5_w_out, (j,), [lambda jb: jb, lambda jb: nd + jb], tn, d, F32,
                res=x, gate=gate_of(l, 1), tm=tm)
        fin = tuple(jnp.swapaxes(f, 0, 1).reshape(bp, 2, g_groups, p_state) for f in fin)
        return x, fin

    new_ml = (jnp.zeros((bp, n_ml, 2, M_HEADS, dk, dv), F32), jnp.zeros((bp, n_ml, 2, M_HEADS, dk), F32),
              jnp.zeros((bp, n_ml, 2, M_HEADS), F32))
    new_re, new_im = [], []
    for l in range(depth):
        x = ffn(x, l, 0, 0)
        if l % 2 == 0:
            x, new_ml = mlstm(x, l, l // 2, new_ml)
        else:
            x, (f_re, f_im) = s5(x, l, l // 2)
            new_re.append(f_re)
            new_im.append(f_im)
        x = ffn(x, l, 1, 2)

    y_prompt = _final_norm(x, final_g, 0, n_ctx).reshape(bp, tp, d)
    y_sample = _final_norm(x, final_g, n_ctx, n_lat).reshape(bs, ts, d)
    return (y_prompt, y_sample) + tuple(new_ml) + (jnp.stack(new_re, axis=1), jnp.stack(new_im, axis=1))
```
